```python
import math
import jax
import jax.numpy as jnp
from jax import lax
import numpy as np

D_MODEL = 4096
BATCH = 4
SEQ = 2048
DEPTH = 1
DEC_BATCH = 128
DEC_SEQ = 8
PAST_LEN = 16384
PAGE_SIZE = 128

NSA_HEADS = 16
NSA_KV_HEADS = 1
NSA_HEAD_DIM = 128
NSA_GROUP = NSA_HEADS // NSA_KV_HEADS
N_NSA_BRANCH = 3
CMP_BLOCK = 32
CMP_STRIDE = 16
PHI_HIDDEN = 256
SLC_BLOCK = 64
N_SELECT = 16
WINDOW = 512
MLA_HEADS = 16
MLA_Q_LORA = 768
MLA_KV_LORA = 512
MLA_NOPE = 128
MLA_ROPE = 64
MLA_V = 128
MLA_SCALE = (MLA_NOPE + MLA_ROPE) ** -0.5
ROPE_THETA = 10000.0
REL_BUCKETS = 32
REL_MAX_DIST = 128
D_FF = 11008
CONV_W = 3
RMS_EPS = 1e-6
Q_BLOCK = 128
N_MERGE = 2
NEG_INF = -1e30
FORCED_SCORE = 1e30
IN_NSA_Q = NSA_HEADS * NSA_HEAD_DIM
IN_NSA_KV = N_NSA_BRANCH * 2 * NSA_KV_HEADS * NSA_HEAD_DIM
IN_NSA_GATE = N_NSA_BRANCH * NSA_HEADS
IN_MLA_QA = MLA_Q_LORA
IN_MLA_KVA = MLA_KV_LORA + MLA_ROPE
IN_MERGE = N_MERGE * D_MODEL
D_IN = IN_NSA_Q + IN_NSA_KV + IN_NSA_GATE + IN_MLA_QA + IN_MLA_KVA + IN_MERGE

kernel_name = 'nsa_mla_gated_hybrid_convffn_step'


def rmsnorm(x, g):
    xf = x.astype(jnp.float32)
    y = xf * lax.rsqrt(jnp.mean(xf * xf, axis=-1, keepdims=True) + RMS_EPS)
    return (y * g.astype(jnp.float32)).astype(x.dtype)


def rope(x, pos):
    half = x.shape[-1] // 2
    inv = ROPE_THETA ** (-jnp.arange(half, dtype=jnp.float32) / half)
    ang = pos.astype(jnp.float32)[..., None] * inv
    cos, sin = jnp.cos(ang), jnp.sin(ang)
    xf = x.astype(jnp.float32)
    x1, x2 = xf[..., :half], xf[..., half:]
    return jnp.concatenate([x1 * cos - x2 * sin, x2 * cos + x1 * sin], axis=-1).astype(x.dtype)


def rel_bucket(dist):
    n = jnp.maximum(dist, 0)
    max_exact = REL_BUCKETS // 2
    nf = jnp.maximum(n, max_exact).astype(jnp.float32)
    large = max_exact + (jnp.log(nf / max_exact) / math.log(REL_MAX_DIST / max_exact)
                         * (REL_BUCKETS - max_exact)).astype(jnp.int32)
    large = jnp.minimum(large, REL_BUCKETS - 1)
    return jnp.where(n < max_exact, n, large)


def masked_softmax(s, valid):
    return jax.nn.softmax(jnp.where(valid, s, NEG_INF), axis=-1)


def nsa_compress(kv, phi_pos, phi_w1, phi_w2):
    t = kv.shape[0]
    n_chunk = -(-t // CMP_STRIDE)
    kv = jnp.pad(kv, ((0, n_chunk * CMP_STRIDE - t), (0, 0), (0, 0), (0, 0)))
    chunks = kv.reshape(n_chunk, CMP_STRIDE, 2, NSA_KV_HEADS, NSA_HEAD_DIM)
    r = CMP_BLOCK // CMP_STRIDE
    nc = n_chunk - r + 1
    blocks = jnp.concatenate([chunks[i:i + nc] for i in range(r)], axis=1)
    blocks = blocks + phi_pos[None, :, :, None, :]
    flat = blocks.transpose(0, 2, 3, 1, 4).reshape(nc, 2, NSA_KV_HEADS, CMP_BLOCK * NSA_HEAD_DIM)
    hid = jax.nn.gelu(jnp.einsum('nckf,cfh->nckh', flat, phi_w1))
    cmp = jnp.einsum('nckh,chd->nckd', hid, phi_w2)
    cmp_end = jnp.arange(nc, dtype=jnp.int32) * CMP_STRIDE + (CMP_BLOCK - 1)
    return cmp, cmp_end


def nsa_block(q, qpos, gates, cmp_kv, cmp_end, slc_kv, win_kv, win_pos, rel_bias):
    nq = q.shape[0]
    qg = (q * NSA_HEAD_DIM ** -0.5).reshape(nq, NSA_KV_HEADS, NSA_GROUP, NSA_HEAD_DIM)

    def shared_bias(kpos):
        bias = rel_bias[rel_bucket(qpos[:, None] - kpos[None, :])]
        return bias.reshape(nq, -1, NSA_KV_HEADS, NSA_GROUP).transpose(0, 2, 3, 1).astype(jnp.float32)

    valid_c = (cmp_end[None, :] <= qpos[:, None])[:, None, None, :]
    s_c = jnp.einsum('qkgd,nkd->qkgn', qg, cmp_kv[:, 0]).astype(jnp.float32) + shared_bias(cmp_end)
    p_c = jnp.where(valid_c, masked_softmax(s_c, valid_c), 0.0)
    o_c = jnp.einsum('qkgn,nkd->qkgd', p_c.astype(q.dtype), cmp_kv[:, 1])

    ns = slc_kv.shape[0] // SLC_BLOCK
    c_start = jnp.arange(cmp_end.shape[0], dtype=jnp.int32) * CMP_STRIDE
    s_start = jnp.arange(ns, dtype=jnp.int32) * SLC_BLOCK
    cover = ((c_start[:, None] < s_start[None, :] + SLC_BLOCK)
             & (c_start[:, None] + CMP_BLOCK > s_start[None, :])).astype(jnp.float32)
    imp = jnp.einsum('qkgn,ns->qks', p_c, cover)
    blk = jnp.arange(ns, dtype=jnp.int32)
    cur = (qpos // SLC_BLOCK)[:, None, None]
    forced = (blk == 0) | (blk == cur) | (blk == cur - 1)
    visible = (blk * SLC_BLOCK) <= qpos[:, None, None]
    score = jnp.where(forced, FORCED_SCORE, jnp.where(visible, imp, -1.0))
    n_sel = min(N_SELECT, ns)
    _, idx = lax.top_k(score, n_sel)

    kvh = jnp.arange(NSA_KV_HEADS)[None, :, None]
    slc_b = slc_kv.reshape(ns, SLC_BLOCK, 2, NSA_KV_HEADS, NSA_HEAD_DIM).transpose(3, 0, 1, 2, 4)
    g_kv = slc_b[kvh, idx].reshape(nq, NSA_KV_HEADS, n_sel * SLC_BLOCK, 2, NSA_HEAD_DIM)
    tok_pos = (idx[..., None] * SLC_BLOCK + jnp.arange(SLC_BLOCK, dtype=jnp.int32)).reshape(nq, NSA_KV_HEADS, -1)
    valid_s = (tok_pos <= qpos[:, None, None])[:, :, None, :]
    tb = rel_bias.reshape(REL_BUCKETS, NSA_KV_HEADS, NSA_GROUP).transpose(1, 0, 2)
    b_s = tb[kvh, rel_bucket(qpos[:, None, None] - tok_pos)].transpose(0, 1, 3, 2).astype(jnp.float32)
    s_s = jnp.einsum('qkgd,qknd->qkgn', qg, g_kv[..., 0, :]).astype(jnp.float32) + b_s
    p_s = masked_softmax(s_s, valid_s)
    o_s = jnp.einsum('qkgn,qknd->qkgd', p_s.astype(q.dtype), g_kv[..., 1, :])

    valid_w = ((win_pos[None, :] <= qpos[:, None]) & (win_pos[None, :] > qpos[:, None] - WINDOW)
               & (win_pos[None, :] >= 0))[:, None, None, :]
    s_w = jnp.einsum('qkgd,wkd->qkgw', qg, win_kv[:, 0]).astype(jnp.float32) + shared_bias(win_pos)
    p_w = masked_softmax(s_w, valid_w)
    o_w = jnp.einsum('qkgw,wkd->qkgd', p_w.astype(q.dtype), win_kv[:, 1])

    g = gates.reshape(nq, NSA_KV_HEADS, NSA_GROUP, N_NSA_BRANCH, 1)
    out = g[..., 0, :] * o_c + g[..., 1, :] * o_s + g[..., 2, :] * o_w
    return out.reshape(nq, NSA_HEADS * NSA_HEAD_DIM)


def nsa_prompt(q, gates, kv, phi_pos, phi_w1, phi_w2, rel_bias):
    s = q.shape[0]
    cmp_kv, cmp_end = nsa_compress(kv[:, 0], phi_pos, phi_w1, phi_w2)
    n_slc = -(-s // SLC_BLOCK)
    slc_kv = jnp.pad(kv[:, 1], ((0, n_slc * SLC_BLOCK - s), (0, 0), (0, 0), (0, 0)))
    win_kv = jnp.pad(kv[:, 2], ((WINDOW, 0), (0, 0), (0, 0), (0, 0)))

    def one_block(i):
        start = i * Q_BLOCK
        q_pos = start + jnp.arange(Q_BLOCK, dtype=jnp.int32)
        w_pos = start - WINDOW + jnp.arange(Q_BLOCK + WINDOW, dtype=jnp.int32)
        return nsa_block(lax.dynamic_slice_in_dim(q, start, Q_BLOCK), q_pos,
                         lax.dynamic_slice_in_dim(gates, start, Q_BLOCK), cmp_kv, cmp_end, slc_kv,
                         lax.dynamic_slice_in_dim(win_kv, start, Q_BLOCK + WINDOW), w_pos, rel_bias)

    out = lax.map(one_block, jnp.arange(s // Q_BLOCK, dtype=jnp.int32))
    return out.reshape(s, NSA_HEADS * NSA_HEAD_DIM)


def nsa_sample(q, gates, kv_new, pages, win_buf, layer, cache_cmp, cache_slc,
               phi_pos, phi_w1, phi_w2, rel_bias):
    ds = q.shape[0]
    past = pages.shape[0] * cache_cmp.shape[2]
    row_shape = (past, 2, NSA_KV_HEADS, NSA_HEAD_DIM)
    cmp_rows = jnp.concatenate([cache_cmp[layer, pages].reshape(row_shape), kv_new[:, 0]], axis=0)
    cmp_kv, cmp_end = nsa_compress(cmp_rows, phi_pos, phi_w1, phi_w2)
    t = past + ds
    n_slc = -(-t // SLC_BLOCK)
    slc_rows = jnp.concatenate([cache_slc[layer, pages].reshape(row_shape), kv_new[:, 1]], axis=0)
    slc_kv = jnp.pad(slc_rows, ((0, n_slc * SLC_BLOCK - t), (0, 0), (0, 0), (0, 0)))
    wb = win_buf.shape[0]
    win_kv = jnp.concatenate([win_buf, kv_new[:, 2]], axis=0)
    w_pos = past - wb + jnp.arange(wb + ds, dtype=jnp.int32)
    q_pos = past + jnp.arange(ds, dtype=jnp.int32)
    return nsa_block(q, q_pos, gates, cmp_kv, cmp_end, slc_kv, win_kv, w_pos, rel_bias)


def mla_project(qa, kva, pos, q_norm, w_uq, kv_norm, w_ukv):
    q = jnp.einsum('...r,rf->...f', rmsnorm(qa, q_norm), w_uq)
    q = q.reshape(*qa.shape[:-1], MLA_HEADS, MLA_NOPE + MLA_ROPE)
    q_nope, q_pe = q[..., :MLA_NOPE], rope(q[..., MLA_NOPE:], pos[..., None])
    q_lat = jnp.einsum('...hn,chn->...hc', q_nope, w_ukv[:, :, :MLA_NOPE])
    ckv = rmsnorm(kva[..., :MLA_KV_LORA], kv_norm)
    kpe = rope(kva[..., MLA_KV_LORA:], pos)
    return q_lat, q_pe, jnp.concatenate([ckv, kpe], axis=-1)


def mla_attend(q_lat, q_pe, qpos, lat, kpos, w_uv):
    s = (jnp.einsum('qhc,kc->hqk', q_lat, lat[:, :MLA_KV_LORA])
         + jnp.einsum('qhr,kr->hqk', q_pe, lat[:, MLA_KV_LORA:])).astype(jnp.float32) * MLA_SCALE
    valid = (kpos[None, :] <= qpos[:, None])[None]
    p = masked_softmax(s, valid)
    o_lat = jnp.einsum('hqk,kc->qhc', p.astype(lat.dtype), lat[:, :MLA_KV_LORA])
    return jnp.einsum('qhc,chv->qhv', o_lat, w_uv).reshape(q_lat.shape[0], MLA_HEADS * MLA_V)


def mla_prompt(q_lat, q_pe, lat, w_uv):
    b, s = q_lat.shape[0], q_lat.shape[1]
    kpos = jnp.arange(s, dtype=jnp.int32)
    attend = jax.vmap(mla_attend, in_axes=(0, 0, None, 0, None, None))

    def one_block(i):
        start = i * Q_BLOCK
        q_pos = start + jnp.arange(Q_BLOCK, dtype=jnp.int32)
        return attend(lax.dynamic_slice_in_dim(q_lat, start, Q_BLOCK, axis=1),
                      lax.dynamic_slice_in_dim(q_pe, start, Q_BLOCK, axis=1), q_pos, lat, kpos, w_uv)

    out = lax.map(one_block, jnp.arange(s // Q_BLOCK, dtype=jnp.int32))
    return out.transpose(1, 0, 2, 3).reshape(b, s, MLA_HEADS * MLA_V)


def mla_sample(q_lat, q_pe, lat_new, pages, layer, cache_mla, w_uv):
    ds = q_lat.shape[0]
    past = pages.shape[0] * cache_mla.shape[2]
    lat_all = jnp.concatenate([cache_mla[layer, pages].reshape(past, MLA_KV_LORA + MLA_ROPE), lat_new], axis=0)
    kpos = jnp.arange(past + ds, dtype=jnp.int32)
    qpos = past + jnp.arange(ds, dtype=jnp.int32)
    return mla_attend(q_lat, q_pe, qpos, lat_all, kpos, w_uv)


def mixer_inputs(x, pos, norm_attn, w_in, q_norm, w_uq, kv_norm, w_ukv):
    xn = rmsnorm(x, norm_attn)
    z = jnp.einsum('bld,df->blf', xn, w_in)
    offs = np.cumsum([IN_NSA_Q, IN_NSA_KV, IN_NSA_GATE, IN_MLA_QA, IN_MLA_KVA])
    z_q, z_kv, z_g, z_qa, z_kva, z_m = jnp.split(z, [int(o) for o in offs], axis=-1)
    lead = x.shape[:-1]
    q_nsa = z_q.reshape(*lead, NSA_HEADS, NSA_HEAD_DIM)
    kv_nsa = z_kv.reshape(*lead, N_NSA_BRANCH, 2, NSA_KV_HEADS, NSA_HEAD_DIM)
    g_nsa = jax.nn.sigmoid(z_g.astype(jnp.float32)).astype(x.dtype).reshape(*lead, NSA_HEADS, N_NSA_BRANCH)
    q_lat, q_pe, lat = mla_project(z_qa, z_kva, pos, q_norm, w_uq, kv_norm, w_ukv)
    return q_nsa, kv_nsa, g_nsa, q_lat, q_pe, lat, z_m


def merge_mixers(o_nsa, o_mla, z_m, w_br_nsa, w_br_mla, w_out):
    g = jax.nn.sigmoid(z_m.astype(jnp.float32)).astype(z_m.dtype)
    y = g[..., :D_MODEL] * (o_nsa @ w_br_nsa) + g[..., D_MODEL:] * (o_mla @ w_br_mla)
    return y @ w_out


def conv_ffn(h, conv_prev, norm_ffn, w_gate, w_up, conv_w, conv_b, w_down):
    hn = rmsnorm(h, norm_ffn)
    a = hn @ w_gate
    u = hn @ w_up
    n = h.shape[1]
    a_ext = jnp.concatenate([conv_prev.astype(a.dtype), a], axis=1)
    conv = conv_b + sum(conv_w[k] * a_ext[:, k:k + n] for k in range(CONV_W))
    out = (jax.nn.silu(conv) * u) @ w_down
    return out, a_ext[:, n:]


def setup_inputs(seed: int = 0) -> dict:
    key = jax.random.key(seed)
    ks = iter(jax.random.split(key, 40))

    def nrm(shape, scale=1.0):
        return jax.random.normal(next(ks), shape, jnp.float32) * scale

    def gain(shape):
        return 1.0 + nrm(shape, 0.01)

    L = DEPTH
    n_pages = PAST_LEN // PAGE_SIZE
    n_used = DEC_BATCH * n_pages
    n_pool = n_used + (n_used + 3) // 4
    win_len = min(WINDOW, PAST_LEN)
    kv_tail = (2, NSA_KV_HEADS, NSA_HEAD_DIM)
    x_prompt = nrm((BATCH, SEQ, D_MODEL))
    x_sample = nrm((DEC_BATCH, DEC_SEQ, D_MODEL))
    cache_mla = nrm((L, n_pool, PAGE_SIZE, MLA_KV_LORA + MLA_ROPE))
    cache_nsa_cmp_kv = nrm((L, n_pool, PAGE_SIZE) + kv_tail)
    cache_nsa_slc_kv = nrm((L, n_pool, PAGE_SIZE) + kv_tail)
    cache_nsa_win_kv = nrm((L, DEC_BATCH, win_len) + kv_tail)
    state_ffn_conv = nrm((L, DEC_BATCH, CONV_W - 1, D_FF))
    page_table = jax.random.permutation(next(ks), n_pool)[:n_used].reshape(DEC_BATCH, n_pages).astype(jnp.int32)
    return {
        'x_prompt': x_prompt,
        'x_sample': x_sample,
        'cache_mla': cache_mla,
        'cache_nsa_cmp_kv': cache_nsa_cmp_kv,
        'cache_nsa_slc_kv': cache_nsa_slc_kv,
        'cache_nsa_win_kv': cache_nsa_win_kv,
        'state_ffn_conv': state_ffn_conv,
        'page_table': page_table,
        'rel_bias': nrm((REL_BUCKETS, NSA_HEADS), 0.1),
        'norm_attn': gain((L, D_MODEL)),
        'w_in': nrm((L, D_MODEL, D_IN), D_MODEL ** -0.5),
        'nsa_phi_pos': nrm((L, CMP_BLOCK, 2, NSA_HEAD_DIM), 0.1),
        'nsa_phi_w1': nrm((L, 2, CMP_BLOCK * NSA_HEAD_DIM, PHI_HIDDEN), (CMP_BLOCK * NSA_HEAD_DIM) ** -0.5),
        'nsa_phi_w2': nrm((L, 2, PHI_HIDDEN, NSA_HEAD_DIM), PHI_HIDDEN ** -0.5),
        'mla_q_norm': gain((L, MLA_Q_LORA)),
        'mla_w_uq': nrm((L, MLA_Q_LORA, MLA_HEADS * (MLA_NOPE + MLA_ROPE)), MLA_Q_LORA ** -0.5),
        'mla_kv_norm': gain((L, MLA_KV_LORA)),
        'mla_w_ukv': nrm((L, MLA_KV_LORA, MLA_HEADS, MLA_NOPE + MLA_V), MLA_KV_LORA ** -0.5),
        'w_br_nsa': nrm((L, NSA_HEADS * NSA_HEAD_DIM, D_MODEL), (NSA_HEADS * NSA_HEAD_DIM) ** -0.5),
        'w_br_mla': nrm((L, MLA_HEADS * MLA_V, D_MODEL), (MLA_HEADS * MLA_V) ** -0.5),
        'w_out': nrm((L, D_MODEL, D_MODEL), D_MODEL ** -0.5),
        'norm_ffn': gain((L, D_MODEL)),
        'w_ffn_gate': nrm((L, D_MODEL, D_FF), D_MODEL ** -0.5),
        'w_ffn_up': nrm((L, D_MODEL, D_FF), D_MODEL ** -0.5),
        'ffn_conv_w': nrm((L, CONV_W, D_FF), CONV_W ** -0.5),
        'ffn_conv_b': nrm((L, D_FF), 0.01),
        'w_ffn_down': nrm((L, D_FF, D_MODEL), D_FF ** -0.5),
        'norm_final': gain((D_MODEL,)),
    }


def reference(x_prompt, x_sample, cache_mla, cache_nsa_cmp_kv, cache_nsa_slc_kv, cache_nsa_win_kv,
              state_ffn_conv, page_table, rel_bias, norm_attn, w_in, nsa_phi_pos, nsa_phi_w1, nsa_phi_w2,
              mla_q_norm, mla_w_uq, mla_kv_norm, mla_w_ukv, w_br_nsa, w_br_mla, w_out, norm_ffn,
              w_ffn_gate, w_ffn_up, ffn_conv_w, ffn_conv_b, w_ffn_down, norm_final):
    b, s, _ = x_prompt.shape
    db, ds, _ = x_sample.shape
    past = page_table.shape[1] * cache_mla.shape[2]
    pos_p = jnp.broadcast_to(jnp.arange(s, dtype=jnp.int32), (b, s))
    pos_s = jnp.broadcast_to(past + jnp.arange(ds, dtype=jnp.int32), (db, ds))
    h_p, h_s = x_prompt, x_sample
    mla_p, mla_s, cmp_p, cmp_s, slc_p, slc_s = [], [], [], [], [], []
    win_p, win_s, conv_p, conv_s = [], [], [], []
    for l in range(DEPTH):
        w_uv = mla_w_ukv[l][:, :, MLA_NOPE:]
        mla_w = (mla_q_norm[l], mla_w_uq[l], mla_kv_norm[l], mla_w_ukv[l])
        phi = (nsa_phi_pos[l], nsa_phi_w1[l], nsa_phi_w2[l])
        ffn_w = (norm_ffn[l], w_ffn_gate[l], w_ffn_up[l], ffn_conv_w[l], ffn_conv_b[l], w_ffn_down[l])

        q_n, kv_n, g_n, q_lat, q_pe, lat, z_m = mixer_inputs(h_p, pos_p, norm_attn[l], w_in[l], *mla_w)
        o_nsa = lax.map(lambda a: nsa_prompt(a[0], a[1], a[2], *phi, rel_bias), (q_n, g_n, kv_n))
        o_mla = mla_prompt(q_lat, q_pe, lat, w_uv)
        h_p = h_p + merge_mixers(o_nsa, o_mla, z_m, w_br_nsa[l], w_br_mla[l], w_out[l])
        f_p, c_p = conv_ffn(h_p, jnp.zeros((b, CONV_W - 1, D_FF), h_p.dtype), *ffn_w)
        h_p = h_p + f_p
        mla_p.append(lat)
        cmp_p.append(kv_n[:, :, 0])
        slc_p.append(kv_n[:, :, 1])
        win_p.append(kv_n[:, s - min(WINDOW, s):, 2])
        conv_p.append(c_p)

        q_n, kv_n, g_n, q_lat, q_pe, lat, z_m = mixer_inputs(h_s, pos_s, norm_attn[l], w_in[l], *mla_w)
        win_buf = cache_nsa_win_kv[l]
        o_nsa = lax.map(lambda a: nsa_sample(a[0], a[1], a[2], a[3], a[4], l, cache_nsa_cmp_kv,
                                             cache_nsa_slc_kv, *phi, rel_bias),
                        (q_n, g_n, kv_n, page_table, win_buf))
        o_mla = lax.map(lambda a: mla_sample(a[0], a[1], a[2], a[3], l, cache_mla, w_uv),
                        (q_lat, q_pe, lat, page_table))
        h_s = h_s + merge_mixers(o_nsa, o_mla, z_m, w_br_nsa[l], w_br_mla[l], w_out[l])
        f_s, c_s = conv_ffn(h_s, state_ffn_conv[l], *ffn_w)
        h_s = h_s + f_s
        wb = win_buf.shape[1]
        win_all = jnp.concatenate([win_buf, kv_n[:, :, 2]], axis=1)
        mla_s.append(lat)
        cmp_s.append(kv_n[:, :, 0])
        slc_s.append(kv_n[:, :, 1])
        win_s.append(win_all[:, wb + ds - min(WINDOW, wb + ds):])
        conv_s.append(c_s)

    y_prompt = rmsnorm(h_p, norm_final)
    y_sample = rmsnorm(h_s, norm_final)
    return (y_prompt, y_sample,
            jnp.stack(mla_p), jnp.stack(mla_s),
            jnp.stack(cmp_p), jnp.stack(cmp_s),
            jnp.stack(slc_p), jnp.stack(slc_s),
            jnp.stack(win_p), jnp.stack(win_s),
            jnp.stack(conv_p), jnp.stack(conv_s))
```

```python
import functools
import math

import numpy as np
import jax
import jax.numpy as jnp
from jax import lax
from jax.experimental import pallas as pl
from jax.experimental.pallas import tpu as pltpu

F32, BF16, I32 = jnp.float32, jnp.bfloat16, jnp.int32

V7X_LANES = 128
V7X_SUBLANES = 8
V7X_VMEM_BYTES = 64 * 1024 * 1024
VMEM_LIMIT = V7X_VMEM_BYTES - 8 * 1024 * 1024

NSA_HEADS = 16
NSA_HEAD_DIM = 128
CMP_BLOCK = 32
CMP_STRIDE = 16
SLC_BLOCK = 64
N_SELECT = 16
WINDOW = 512
MLA_HEADS = 16
MLA_Q_LORA = 768
MLA_KV_LORA = 512
MLA_NOPE = 128
MLA_ROPE = 64
MLA_V = 128
MLA_SCALE = (MLA_NOPE + MLA_ROPE) ** -0.5
ROPE_THETA = 10000.0
REL_BUCKETS = 32
REL_MAX_DIST = 128
RMS_EPS = 1e-6
Q_BLOCK = 128
NEG_INF = -1e30
FORCED_SCORE = 1e30
PAD_SCORE = -3e38

Z_Q = 0
Z_G = 2048
Z_QA = 2304
Z_KV = 3072
Z_KPE = 3840
Z_CKV = 4096
Z_MA = 4608
Z_MB = 8704
Z_WIDTH = 12800


def _cparams(sem, vmem=VMEM_LIMIT):
    return pltpu.CompilerParams(dimension_semantics=sem, vmem_limit_bytes=vmem)


def _dot(a, b):
    return jnp.dot(a, b, preferred_element_type=F32)


def _dot_nt(a, b):
    return lax.dot_general(a, b, (((1,), (1,)), ((), ())), preferred_element_type=F32)


def _split_bf16(x):
    hi = x.astype(BF16)
    lo = (x - hi.astype(F32)).astype(BF16)
    return hi, lo


def _rmsnorm_body(x_ref, g_ref, o_ref):
    x = x_ref[...]
    ms = jnp.mean(x * x, axis=-1, keepdims=True)
    o_ref[...] = (x * lax.rsqrt(ms + RMS_EPS) * g_ref[...]).astype(o_ref.dtype)


def _rmsnorm(x, g, out_dtype, *, row0=0, rows=None, tm=256, name="rmsnorm"):
    m_all, d = x.shape
    rows = m_all if rows is None else rows
    assert rows % tm == 0 and row0 % tm == 0
    b0 = row0 // tm
    return pl.pallas_call(
        _rmsnorm_body,
        grid=(rows // tm,),
        in_specs=[pl.BlockSpec((tm, d), lambda i: (b0 + i, 0)),
                  pl.BlockSpec((1, d), lambda i: (0, 0))],
        out_specs=pl.BlockSpec((tm, d), lambda i: (i, 0)),
        out_shape=jax.ShapeDtypeStruct((rows, d), out_dtype),
        compiler_params=_cparams(("parallel",)),
        name=name,
    )(x, g.reshape(1, d))


def _mm_body(*refs, nk, tk, k_valid, n_extra, epilogue):
    a_ref, w_ref = refs[0], refs[1]
    extra = refs[2:2 + n_extra]
    o_ref, acc_ref = refs[2 + n_extra], refs[3 + n_extra]
    k = pl.program_id(2)

    @pl.when(k == 0)
    def _init():
        acc_ref[...] = jnp.zeros_like(acc_ref)

    a = a_ref[...].astype(BF16)
    w = w_ref[...].astype(BF16)
    if k_valid % tk:
        row = lax.broadcasted_iota(I32, w.shape, 0) + k * tk
        w = jnp.where(row < k_valid, w, jnp.zeros_like(w))
        col = lax.broadcasted_iota(I32, a.shape, 1) + k * tk
        a = jnp.where(col < k_valid, a, jnp.zeros_like(a))
    acc_ref[...] += _dot(a, w)

    @pl.when(k == nk - 1)
    def _fin():
        o_ref[...] = epilogue(acc_ref[...], *(e[...] for e in extra)).astype(o_ref.dtype)


def _matmul(a, w, *, tm, tn, tk, out_dtype, a_col0=0, extras=(), epilogue=None, name):
    m = a.shape[0]
    k_valid, n = w.shape
    nk = pl.cdiv(k_valid, tk)
    assert m % tm == 0 and n % tn == 0 and a_col0 % tk == 0
    assert a.shape[1] >= a_col0 + k_valid
    if epilogue is None:
        epilogue = lambda acc: acc
    ka0 = a_col0 // tk
    in_specs = [pl.BlockSpec((tm, tk), lambda i, j, k: (i, ka0 + k)),
                pl.BlockSpec((tk, tn), lambda i, j, k: (k, j))]
    args = [a, w]
    for arr, col0, *rest in extras:
        row0 = rest[0] if rest else 0
        assert col0 % tn == 0 and row0 % tm == 0
        in_specs.append(pl.BlockSpec((tm, tn), lambda i, j, k, c=col0 // tn, r=row0 // tm: (r + i, c + j)))
        args.append(arr)
    body = functools.partial(_mm_body, nk=nk, tk=tk, k_valid=k_valid, n_extra=len(extras),
                             epilogue=epilogue)
    return pl.pallas_call(
        body,
        grid=(m // tm, n // tn, nk),
        in_specs=in_specs,
        out_specs=pl.BlockSpec((tm, tn), lambda i, j, k: (i, j)),
        out_shape=jax.ShapeDtypeStruct((m, n), out_dtype),
        scratch_shapes=[pltpu.VMEM((tm, tn), F32)],
        compiler_params=_cparams(("parallel", "parallel", "arbitrary")),
        name=name,
    )(*args)


def _rope_tile(x, cs, sn):
    lane = lax.broadcasted_iota(I32, x.shape, x.ndim - 1) & (V7X_LANES - 1)
    half = MLA_ROPE // 2
    rot = jnp.where(lane < half, pltpu.roll(x, V7X_LANES - half, x.ndim - 1),
                    pltpu.roll(x, half, x.ndim - 1))
    return x * cs + rot * sn


def _mla_proj_body(qa_ref, kpe_ref, ckv_ref, cs_ref, sn_ref, qn_ref, wuq_ref, kvn_ref, wk_ref,
                   qlat_ref, qpe_ref, lat_ref):
    qa = qa_ref[...]
    ms = jnp.mean(qa * qa, axis=-1, keepdims=True)
    qan = (qa * lax.rsqrt(ms + RMS_EPS) * qn_ref[...]).astype(BF16)
    q = _dot(qan, wuq_ref[...])
    nope_w = MLA_HEADS * MLA_NOPE
    for h in range(MLA_HEADS):
        qn_h = q[:, h * MLA_NOPE:(h + 1) * MLA_NOPE].astype(BF16)
        qlat_ref[:, h * MLA_KV_LORA:(h + 1) * MLA_KV_LORA] = _dot(qn_h, wk_ref[h])
    cs = cs_ref[...]
    sn = sn_ref[...]
    for h in range(MLA_HEADS):
        sl = slice(h * V7X_LANES, (h + 1) * V7X_LANES)
        qpe_ref[:, sl] = _rope_tile(q[:, nope_w + h * V7X_LANES:nope_w + (h + 1) * V7X_LANES], cs, sn)
    ckv = ckv_ref[...]
    ms2 = jnp.mean(ckv * ckv, axis=-1, keepdims=True)
    lat_ref[:, 0:MLA_KV_LORA] = ckv * lax.rsqrt(ms2 + RMS_EPS) * kvn_ref[...]
    kpe = _rope_tile(kpe_ref[...], cs, sn)
    lat_ref[:, MLA_KV_LORA:MLA_KV_LORA + MLA_ROPE] = kpe[:, 0:MLA_ROPE]


def _mla_project(z, cs, sn, q_norm, w_uq_p, kv_norm, wk, *, tm=256):
    m = z.shape[0]
    lat_w = MLA_KV_LORA + MLA_ROPE
    return pl.pallas_call(
        _mla_proj_body,
        grid=(m // tm,),
        in_specs=[
            pl.BlockSpec((tm, MLA_Q_LORA), lambda i: (i, Z_QA // MLA_Q_LORA)),
            pl.BlockSpec((tm, V7X_LANES), lambda i: (i, Z_KPE // V7X_LANES)),
            pl.BlockSpec((tm, MLA_KV_LORA), lambda i: (i, Z_CKV // MLA_KV_LORA)),
            pl.BlockSpec((tm, V7X_LANES), lambda i: (i, 0)),
            pl.BlockSpec((tm, V7X_LANES), lambda i: (i, 0)),
            pl.BlockSpec((1, MLA_Q_LORA), lambda i: (0, 0)),
            pl.BlockSpec(w_uq_p.shape, lambda i: (0, 0)),
            pl.BlockSpec((1, MLA_KV_LORA), lambda i: (0, 0)),
            pl.BlockSpec(wk.shape, lambda i: (0, 0, 0)),
        ],
        out_specs=[
            pl.BlockSpec((tm, MLA_HEADS * MLA_KV_LORA), lambda i: (i, 0)),
            pl.BlockSpec((tm, MLA_HEADS * V7X_LANES), lambda i: (i, 0)),
            pl.BlockSpec((tm, lat_w), lambda i: (i, 0)),
        ],
        out_shape=[
            jax.ShapeDtypeStruct((m, MLA_HEADS * MLA_KV_LORA), F32),
            jax.ShapeDtypeStruct((m, MLA_HEADS * V7X_LANES), F32),
            jax.ShapeDtypeStruct((m, lat_w), F32),
        ],
        compiler_params=_cparams(("parallel",)),
        name="mla_project",
    )(z, z, z, cs, sn, q_norm.reshape(1, -1), w_uq_p, kv_norm.reshape(1, -1), wk)


def _stack_heads(ref, width, valid=None):
    valid = width if valid is None else valid
    return jnp.concatenate(
        [ref[:, h * width:h * width + valid].astype(BF16) for h in range(MLA_HEADS)], axis=0)


def _mla_prompt_body(qlat_ref, qpe_ref, lat_ref, o_ref, qlat_s, qpe_s, m_s, l_s, acc_s, *, tk):
    qb = pl.program_id(1)
    nq = qlat_ref.shape[0]
    rows = MLA_HEADS * nq
    qlat_s[...] = _stack_heads(qlat_ref, MLA_KV_LORA)
    qpe_s[...] = _stack_heads(qpe_ref, V7X_LANES, MLA_ROPE)
    m_s[...] = jnp.full(m_s.shape, NEG_INF, F32)
    l_s[...] = jnp.zeros(l_s.shape, F32)
    acc_s[...] = jnp.zeros(acc_s.shape, F32)
    qpos = qb * nq + (lax.broadcasted_iota(I32, (rows, tk), 0) & (nq - 1))
    lane = lax.broadcasted_iota(I32, (rows, tk), 1)

    def step(kt, carry):
        k0 = pl.multiple_of(kt * tk, tk)
        kv = lat_ref[pl.ds(k0, tk), :]
        ckv = kv[:, 0:MLA_KV_LORA].astype(BF16)
        kpe = kv[:, MLA_KV_LORA:MLA_KV_LORA + MLA_ROPE].astype(BF16)
        s = (_dot_nt(qlat_s[...], ckv) + _dot_nt(qpe_s[...], kpe)) * MLA_SCALE
        valid = (k0 + lane) <= qpos
        s = jnp.where(valid, s, NEG_INF)
        m_prev = m_s[...]
        m_new = jnp.maximum(m_prev, jnp.max(s, axis=1, keepdims=True))
        alpha = jnp.exp(m_prev - m_new)
        p = jnp.where(valid, jnp.exp(s - m_new), 0.0)
        l_s[...] = alpha * l_s[...] + jnp.sum(p, axis=1, keepdims=True)
        acc_s[...] = alpha * acc_s[...] + _dot(p.astype(BF16), ckv)
        m_s[...] = m_new
        return carry

    n_tiles = ((qb + 1) * nq + tk - 1) // tk
    lax.fori_loop(0, n_tiles, step, 0)
    o = acc_s[...] / l_s[...]
    for h in range(MLA_HEADS):
        o_ref[:, h * MLA_KV_LORA:(h + 1) * MLA_KV_LORA] = o[h * nq:(h + 1) * nq, :]


def _mla_prompt(q_lat, q_pe, lat, *, batch, seq, tk=256):
    nqb = seq // Q_BLOCK
    rows = MLA_HEADS * Q_BLOCK
    lat_w = MLA_KV_LORA + MLA_ROPE
    return pl.pallas_call(
        functools.partial(_mla_prompt_body, tk=tk),
        grid=(batch, nqb),
        in_specs=[
            pl.BlockSpec((Q_BLOCK, MLA_HEADS * MLA_KV_LORA), lambda b, q: (b * nqb + q, 0)),
            pl.BlockSpec((Q_BLOCK, MLA_HEADS * V7X_LANES), lambda b, q: (b * nqb + q, 0)),
            pl.BlockSpec((seq, lat_w), lambda b, q: (b, 0)),
        ],
        out_specs=pl.BlockSpec((Q_BLOCK, MLA_HEADS * MLA_KV_LORA), lambda b, q: (b * nqb + q, 0)),
        out_shape=jax.ShapeDtypeStruct((batch * seq, MLA_HEADS * MLA_KV_LORA), F32),
        scratch_shapes=[
            pltpu.VMEM((rows, MLA_KV_LORA), BF16),
            pltpu.VMEM((rows, MLA_ROPE), BF16),
            pltpu.VMEM((rows, 1), F32),
            pltpu.VMEM((rows, 1), F32),
            pltpu.VMEM((rows, MLA_KV_LORA), F32),
        ],
        compiler_params=_cparams(("parallel", "arbitrary")),
        name="mla_prompt",
    )(q_lat, q_pe, lat)


def _page_gather(pt_ref, cache_ref, buf, sem, seq, page0, n_pages, slot, page, start, split=None):
    def body(p, carry):
        src = cache_ref.at[pt_ref[seq, page0 + p]] if start else cache_ref.at[0]
        rows = pl.ds(p * page, page)
        if split is None:
            copies = [pltpu.make_async_copy(src, buf.at[slot, rows], sem.at[slot])]
        else:
            copies = [pltpu.make_async_copy(src.at[:, c * split:(c + 1) * split], buf.at[slot, c, rows],
                                            sem.at[slot]) for c in range(2)]
        for cp in copies:
            if start:
                cp.start()
            else:
                cp.wait()
        return carry
    lax.fori_loop(0, n_pages, body, 0)


def _mla_sample_body(pt_ref, qlat_ref, qpe_ref, new_ref, cache_ref, o_ref,
                     buf, sem, qlat_s, qpe_s, new_s, m_s, l_s, acc_s,
                     *, n_seq, n_chunks, ppc, page, sub):
    s_id = pl.program_id(0)
    c_id = pl.program_id(1)
    g = s_id * n_chunks + c_id
    slot = g % 2
    ds = qlat_ref.shape[0]
    rows = MLA_HEADS * ds

    @pl.when(g == 0)
    def _first():
        _page_gather(pt_ref, cache_ref, buf, sem, 0, 0, ppc, 0, page, True)

    @pl.when(g + 1 < n_seq * n_chunks)
    def _prefetch():
        g1 = g + 1
        _page_gather(pt_ref, cache_ref, buf, sem, g1 // n_chunks, (g1 % n_chunks) * ppc, ppc,
                     g1 % 2, page, True)

    @pl.when(c_id == 0)
    def _init():
        qlat_s[...] = _stack_heads(qlat_ref, MLA_KV_LORA)
        qpe_s[...] = _stack_heads(qpe_ref, V7X_LANES, MLA_ROPE)
        m_s[...] = jnp.full(m_s.shape, NEG_INF, F32)
        l_s[...] = jnp.zeros(l_s.shape, F32)
        acc_s[...] = jnp.zeros(acc_s.shape, F32)

    _page_gather(pt_ref, cache_ref, buf, sem, s_id, 0, ppc, slot, page, False)

    def flash(kv, valid):
        ckv = kv[:, 0:MLA_KV_LORA].astype(BF16)
        kpe = kv[:, MLA_KV_LORA:MLA_KV_LORA + MLA_ROPE].astype(BF16)
        s = (_dot_nt(qlat_s[...], ckv) + _dot_nt(qpe_s[...], kpe)) * MLA_SCALE
        if valid is not None:
            s = jnp.where(valid, s, NEG_INF)
        m_prev = m_s[...]
        m_new = jnp.maximum(m_prev, jnp.max(s, axis=1, keepdims=True))
        alpha = jnp.exp(m_prev - m_new)
        p = jnp.exp(s - m_new)
        if valid is not None:
            p = jnp.where(valid, p, 0.0)
        l_s[...] = alpha * l_s[...] + jnp.sum(p, axis=1, keepdims=True)
        acc_s[...] = alpha * acc_s[...] + _dot(p.astype(BF16), ckv)
        m_s[...] = m_new

    def step(j, carry):
        k0 = pl.multiple_of(j * sub, sub)
        flash(buf[slot, pl.ds(k0, sub), :], None)
        return carry

    lax.fori_loop(0, (ppc * page) // sub, step, 0)

    @pl.when(c_id == n_chunks - 1)
    def _finish():
        new_s[...] = jnp.zeros(new_s.shape, F32)
        new_s[0:ds, :] = new_ref[...]
        nk = new_s.shape[0]
        t_q = lax.broadcasted_iota(I32, (rows, nk), 0) & (ds - 1)
        j_k = lax.broadcasted_iota(I32, (rows, nk), 1)
        flash(new_s[...], j_k <= t_q)
        o = acc_s[...] / l_s[...]
        for h in range(MLA_HEADS):
            o_ref[:, h * MLA_KV_LORA:(h + 1) * MLA_KV_LORA] = o[h * ds:(h + 1) * ds, :]


def _mla_sample(page_table, q_lat, q_pe, lat, cache, *, row0, ds, ppc=32, sub=1024):
    n_seq, n_pages = page_table.shape
    page = cache.shape[1]
    lat_w = cache.shape[2]
    assert n_pages % ppc == 0 and (ppc * page) % sub == 0 and row0 % ds == 0
    n_chunks = n_pages // ppc
    rows = MLA_HEADS * ds
    r0 = row0 // ds
    body = functools.partial(_mla_sample_body, n_seq=n_seq, n_chunks=n_chunks, ppc=ppc, page=page,
                             sub=sub)
    grid_spec = pltpu.PrefetchScalarGridSpec(
        num_scalar_prefetch=1,
        grid=(n_seq, n_chunks),
        in_specs=[
            pl.BlockSpec((ds, MLA_HEADS * MLA_KV_LORA), lambda s, c, pt: (r0 + s, 0)),
            pl.BlockSpec((ds, MLA_HEADS * V7X_LANES), lambda s, c, pt: (r0 + s, 0)),
            pl.BlockSpec((ds, lat_w), lambda s, c, pt: (r0 + s, 0)),
            pl.BlockSpec(memory_space=pl.ANY),
        ],
        out_specs=pl.BlockSpec((ds, MLA_HEADS * MLA_KV_LORA), lambda s, c, pt: (s, 0)),
        scratch_shapes=[
            pltpu.VMEM((2, ppc * page, lat_w), F32),
            pltpu.SemaphoreType.DMA((2,)),
            pltpu.VMEM((rows, MLA_KV_LORA), BF16),
            pltpu.VMEM((rows, MLA_ROPE), BF16),
            pltpu.VMEM((V7X_LANES, lat_w), F32),
            pltpu.VMEM((rows, 1), F32),
            pltpu.VMEM((rows, 1), F32),
            pltpu.VMEM((rows, MLA_KV_LORA), F32),
        ],
    )
    return pl.pallas_call(
        body,
        grid_spec=grid_spec,
        out_shape=jax.ShapeDtypeStruct((n_seq * ds, MLA_HEADS * MLA_KV_LORA), F32),
        compiler_params=_cparams(("arbitrary", "arbitrary")),
        name="mla_sample",
    )(page_table, q_lat, q_pe, lat, cache)


def _mla_uv_body(o_ref, w_ref, out_ref):
    for h in range(MLA_HEADS):
        o_h = o_ref[:, h * MLA_KV_LORA:(h + 1) * MLA_KV_LORA].astype(BF16)
        out_ref[:, h * MLA_V:(h + 1) * MLA_V] = _dot(o_h, w_ref[h]).astype(out_ref.dtype)


def _mla_uv(o_lat, w_uv, *, tm=256, name="mla_uv"):
    m = o_lat.shape[0]
    assert m % tm == 0
    return pl.pallas_call(
        _mla_uv_body,
        grid=(m // tm,),
        in_specs=[pl.BlockSpec((tm, MLA_HEADS * MLA_KV_LORA), lambda i: (i, 0)),
                  pl.BlockSpec(w_uv.shape, lambda i: (0, 0, 0))],
        out_specs=pl.BlockSpec((tm, MLA_HEADS * MLA_V), lambda i: (i, 0)),
        out_shape=jax.ShapeDtypeStruct((m, MLA_HEADS * MLA_V), BF16),
        compiler_params=_cparams(("parallel",)),
        name=name,
    )(o_lat, w_uv)


def _rel_bucket_np(dist):
    n = np.maximum(dist, 0)
    max_exact = REL_BUCKETS // 2
    nf = np.maximum(n, max_exact).astype(np.float32)
    large = max_exact + (np.log(nf / max_exact) / math.log(REL_MAX_DIST / max_exact)
                         * (REL_BUCKETS - max_exact)).astype(np.int32)
    large = np.minimum(large, REL_BUCKETS - 1)
    return np.where(n < max_exact, n, large).astype(np.int32)


def _bias_bank_body(rb_ref, bucket_ref, o_ref):
    h = pl.program_id(0)
    bucket = bucket_ref[...]
    acc = jnp.zeros(bucket.shape, F32)
    for b in range(REL_BUCKETS):
        acc = jnp.where(bucket == b, rb_ref[b, h], acc)
    o_ref[0] = acc


def _bias_bank(rel_bias, bucket, name):
    r, c = bucket.shape
    return pl.pallas_call(
        _bias_bank_body,
        grid=(NSA_HEADS,),
        in_specs=[pl.BlockSpec(memory_space=pltpu.SMEM),
                  pl.BlockSpec((r, c), lambda h: (0, 0))],
        out_specs=pl.BlockSpec((1, r, c), lambda h: (h, 0, 0)),
        out_shape=jax.ShapeDtypeStruct((NSA_HEADS, r, c), F32),
        compiler_params=_cparams(("parallel",)),
        name=name,
    )(rel_bias, jnp.asarray(bucket))


def _gelu_tanh(x):
    return 0.5 * x * (1.0 + jnp.tanh(math.sqrt(2.0 / math.pi) * (x + 0.044715 * x * x * x)))


def _compress_rows(rows_ref, n_blocks, phi_ref, w1_ref, w2_ref, flat_s, outs):
    hd = NSA_HEAD_DIM
    gb = flat_s.shape[0]
    for c in range(2):
        for g0 in range(0, n_blocks, gb):
            for i in range(CMP_BLOCK):
                x = rows_ref[c, pl.ds(g0 * CMP_STRIDE + i, gb, stride=CMP_STRIDE), :]
                flat_s[:, i * hd:(i + 1) * hd] = (x + phi_ref[i:i + 1, c * hd:(c + 1) * hd]).astype(BF16)
            hid = _gelu_tanh(_dot(flat_s[...], w1_ref[c]))
            outs[c][0, g0:g0 + gb, :] = _dot(hid.astype(BF16), w2_ref[c].astype(BF16))


def _cmp_prompt_body(x_ref, phi_ref, w1_ref, w2_ref, ck_ref, cv_ref, xpad_s, flat_s, *, n_blocks):
    seq = x_ref.shape[0]
    hd = NSA_HEAD_DIM
    for c in range(2):
        xpad_s[c, 0:seq, :] = x_ref[:, c * hd:(c + 1) * hd]
        xpad_s[c, seq:, :] = jnp.zeros((xpad_s.shape[1] - seq, hd), F32)
    _compress_rows(xpad_s, n_blocks, phi_ref, w1_ref, w2_ref, flat_s, (ck_ref, cv_ref))


def _cmp_prompt(z, phi, w1, w2, *, batch, seq):
    hd = NSA_HEAD_DIM
    n_blocks = seq // CMP_STRIDE
    pad_rows = n_blocks * CMP_STRIDE + CMP_BLOCK
    out = jax.ShapeDtypeStruct((batch, n_blocks, hd), F32)
    return pl.pallas_call(
        functools.partial(_cmp_prompt_body, n_blocks=n_blocks),
        grid=(batch,),
        in_specs=[
            pl.BlockSpec((seq, 2 * hd), lambda b: (b, Z_KV // (2 * hd))),
            pl.BlockSpec(phi.shape, lambda b: (0, 0)),
            pl.BlockSpec(w1.shape, lambda b: (0, 0, 0)),
            pl.BlockSpec(w2.shape, lambda b: (0, 0, 0)),
        ],
        out_specs=[pl.BlockSpec((1, n_blocks, hd), lambda b: (b, 0, 0))] * 2,
        out_shape=[out, out],
        scratch_shapes=[pltpu.VMEM((2, pad_rows, hd), F32),
                        pltpu.VMEM((n_blocks, CMP_BLOCK * hd), BF16)],
        compiler_params=_cparams(("parallel",)),
        name="nsa_cmp_prompt",
    )(z, phi, w1, w2)


def _cmp_sample_body(pt_ref, new_ref, cache_ref, phi_ref, w1_ref, w2_ref, ck_ref, cv_ref,
                     buf, sem, flat_s, *, n_seq, n_pages, page, n_blocks):
    s_id = pl.program_id(0)
    slot = s_id % 2
    past = n_pages * page
    ds = new_ref.shape[0]

    hd = NSA_HEAD_DIM

    @pl.when(s_id == 0)
    def _first():
        _page_gather(pt_ref, cache_ref, buf, sem, 0, 0, n_pages, 0, page, True, split=hd)

    @pl.when(s_id + 1 < n_seq)
    def _prefetch():
        _page_gather(pt_ref, cache_ref, buf, sem, s_id + 1, 0, n_pages,
                     (s_id + 1) % 2, page, True, split=hd)

    tail = buf.shape[2] - past
    for c in range(2):
        buf[slot, c, past:past + ds, :] = new_ref[:, c * hd:(c + 1) * hd]
        buf[slot, c, past + ds:, :] = jnp.zeros((tail - ds, hd), F32)
    _page_gather(pt_ref, cache_ref, buf, sem, s_id, 0, n_pages, slot, page, False, split=hd)
    _compress_rows(buf.at[slot], n_blocks, phi_ref, w1_ref, w2_ref, flat_s, (ck_ref, cv_ref))


def _cmp_sample(page_table, z, cache, phi, w1, w2, *, row0, ds):
    n_seq, n_pages = page_table.shape
    page = cache.shape[1]
    hd = NSA_HEAD_DIM
    past = n_pages * page
    n_chunk = -(-(past + ds) // CMP_STRIDE)
    n_blocks = n_chunk - CMP_BLOCK // CMP_STRIDE + 1
    assert n_blocks % V7X_SUBLANES == 0 and row0 % ds == 0
    buf_rows = n_chunk * CMP_STRIDE
    r0 = row0 // ds
    out = jax.ShapeDtypeStruct((n_seq, n_blocks, hd), F32)
    grid_spec = pltpu.PrefetchScalarGridSpec(
        num_scalar_prefetch=1,
        grid=(n_seq,),
        in_specs=[
            pl.BlockSpec((ds, 2 * hd), lambda s, pt: (r0 + s, Z_KV // (2 * hd))),
            pl.BlockSpec(memory_space=pl.ANY),
            pl.BlockSpec(phi.shape, lambda s, pt: (0, 0)),
            pl.BlockSpec(w1.shape, lambda s, pt: (0, 0, 0)),
            pl.BlockSpec(w2.shape, lambda s, pt: (0, 0, 0)),
        ],
        out_specs=[pl.BlockSpec((1, n_blocks, hd), lambda s, pt: (s, 0, 0))] * 2,
        scratch_shapes=[
            pltpu.VMEM((2, 2, buf_rows, hd), F32),
            pltpu.SemaphoreType.DMA((2,)),
            pltpu.VMEM((min(n_blocks, 512), CMP_BLOCK * hd), BF16),
        ],
    )
    body = functools.partial(_cmp_sample_body, n_seq=n_seq, n_pages=n_pages, page=page,
                             n_blocks=n_blocks)
    return pl.pallas_call(
        body,
        grid_spec=grid_spec,
        out_shape=[out, out],
        compiler_params=_cparams(("arbitrary",)),
        name="nsa_cmp_sample",
    )(page_table, z, cache, phi, w1, w2)


def _softmax_rows(s, valid):
    s = jnp.where(valid, s, NEG_INF)
    m = jnp.max(s, axis=-1, keepdims=True)
    p = jnp.where(valid, jnp.exp(s - m), 0.0)
    l = jnp.sum(p, axis=-1, keepdims=True)
    return p / jnp.where(l > 0.0, l, 1.0)


def _flash_update(s, valid, v, m_s, l_s, acc_s):
    nh, nq, tk = s.shape
    s = jnp.where(valid, s, NEG_INF)
    m_prev = m_s[...]
    m_new = jnp.maximum(m_prev, jnp.max(s, axis=-1, keepdims=True))
    alpha = jnp.exp(m_prev - m_new)
    p = jnp.where(valid, jnp.exp(s - m_new), 0.0)
    l_s[...] = alpha * l_s[...] + jnp.sum(p, axis=-1, keepdims=True)
    pv = _dot(p.astype(BF16).reshape(nh * nq, tk), v)
    acc_s[...] = alpha * acc_s[...] + pv.reshape(nh, nq, v.shape[1])
    m_s[...] = m_new


def _flash_reset(m_s, l_s, acc_s):
    m_s[...] = jnp.full(m_s.shape, NEG_INF, F32)
    l_s[...] = jnp.zeros(l_s.shape, F32)
    acc_s[...] = jnp.zeros(acc_s.shape, F32)


def _nsa_prompt_body(q_ref, g_ref, ck_ref, cv_ref, slc_ref, win_ref, bank_ref, bankc_ref,
                     cover_ref, expand_ref, o_ref,
                     qs_s, sel_s, oc_s, m_s, l_s, acc_s, os_s, *, tk, n_slc_blocks):
    qb = pl.program_id(1)
    nq = q_ref.shape[0]
    hd = NSA_HEAD_DIM
    nh = NSA_HEADS
    rows = nh * nq
    start = qb * nq
    scale = hd ** -0.5
    qs_s[...] = jnp.concatenate(
        [(q_ref[:, h * hd:(h + 1) * hd] * scale).astype(BF16) for h in range(nh)], axis=0)

    nc = ck_ref.shape[1]
    s_c = _dot_nt(qs_s[...], ck_ref[0].astype(BF16)) + bankc_ref[...].reshape(rows, nc)
    i_row = lax.broadcasted_iota(I32, (rows, nc), 0) & (nq - 1)
    n_col = lax.broadcasted_iota(I32, (rows, nc), 1)
    valid_c = (n_col * CMP_STRIDE + (CMP_BLOCK - 1)) <= (start + i_row)
    p_c = _softmax_rows(s_c, valid_c)
    oc_s[...] = _dot(p_c.astype(BF16), cv_ref[0].astype(BF16))

    ph = p_c[0:nq, :]
    for h in range(1, nh):
        ph = ph + p_c[h * nq:(h + 1) * nq, :]
    hi, lo = _split_bf16(ph)
    imp = _dot(hi, cover_ref[...]) + _dot(lo, cover_ref[...])
    ns_pad = imp.shape[1]
    blk = lax.broadcasted_iota(I32, (nq, ns_pad), 1)
    qpos = start + lax.broadcasted_iota(I32, (nq, ns_pad), 0)
    cur = lax.shift_right_logical(qpos, int(math.log2(SLC_BLOCK)))
    forced = (blk == 0) | (blk == cur) | (blk == cur - 1)
    visible = (blk * SLC_BLOCK) <= qpos
    score = jnp.where(forced, FORCED_SCORE, jnp.where(visible, imp, -1.0))
    score = jnp.where(blk < n_slc_blocks, score, PAD_SCORE)
    rank = jnp.zeros((nq, ns_pad), F32)
    for sp in range(n_slc_blocks):
        col = score[:, sp:sp + 1]
        ahead = (col > score) | ((col == score) & (blk > sp))
        rank = rank + jnp.where(ahead, 1.0, 0.0)
    sel = jnp.where((rank < float(min(N_SELECT, n_slc_blocks))) & (blk < n_slc_blocks), 1.0, 0.0)
    selx = _dot(sel.astype(BF16), expand_ref[...])
    n_kt = selx.shape[1] // tk
    for kt in range(n_kt):
        sel_s[kt] = selx[:, kt * tk:(kt + 1) * tk]

    i3 = lax.broadcasted_iota(I32, (nq, tk), 0)
    j3 = lax.broadcasted_iota(I32, (nq, tk), 1)
    n_bank = bank_ref.shape[0]
    half = tk // V7X_LANES

    def bias_tile(r0_tiles):
        parts = [bank_ref[jnp.clip(r0_tiles + u, 0, n_bank - 1)] for u in range(half)]
        return jnp.concatenate(parts, axis=1)

    _flash_reset(m_s, l_s, acc_s)
    assert nq == V7X_LANES
    org_tiles = qb - (WINDOW + Q_BLOCK) // V7X_LANES

    def slc_step(kt, carry):
        k0 = pl.multiple_of(kt * tk, tk)
        kv = slc_ref[pl.ds(k0, tk), :]
        s = _dot_nt(qs_s[...], kv[:, 0:hd].astype(BF16))
        s = s + bias_tile(kt * half - org_tiles)
        valid = (sel_s[kt] > 0.5) & ((k0 + j3) <= (start + i3))
        s3 = s.reshape(nh, nq, tk)
        _flash_update(s3, valid[None], kv[:, hd:2 * hd].astype(BF16), m_s, l_s, acc_s)
        return carry

    lax.fori_loop(0, (start + nq + tk - 1) // tk, slc_step, 0)
    os_s[...] = acc_s[...] / l_s[...]

    _flash_reset(m_s, l_s, acc_s)
    pad = WINDOW + Q_BLOCK
    for d in range(-(-(pad + nq) // tk)):
        r0 = d * tk
        kv = win_ref[0, pl.ds(start + r0, tk), :]
        kpos = start - pad + r0 + j3
        s = _dot_nt(qs_s[...], kv[:, 0:hd].astype(BF16)) + bias_tile(d * half)
        qp = start + i3
        valid = (kpos <= qp) & (kpos > qp - WINDOW) & (kpos >= 0)
        _flash_update(s.reshape(nh, nq, tk), valid[None], kv[:, hd:2 * hd].astype(BF16),
                      m_s, l_s, acc_s)
    o_w = acc_s[...] / l_s[...]

    gate = jax.nn.sigmoid(g_ref[...])
    for h in range(nh):
        o_ref[:, h * hd:(h + 1) * hd] = (
            gate[:, 3 * h:3 * h + 1] * oc_s[h * nq:(h + 1) * nq, :]
            + gate[:, 3 * h + 1:3 * h + 2] * os_s[h]
            + gate[:, 3 * h + 2:3 * h + 3] * o_w[h])


def _nsa_prompt(z, cmp_k, cmp_v, win_pad, bank, bank_c, cover, expand, *, batch, seq, tk=256):
    nqb = seq // Q_BLOCK
    hd = NSA_HEAD_DIM
    nh = NSA_HEADS
    rows = nh * Q_BLOCK
    nc = cmp_k.shape[1]
    n_slc_blocks = -(-seq // SLC_BLOCK)
    body = functools.partial(_nsa_prompt_body, tk=tk, n_slc_blocks=n_slc_blocks)
    return pl.pallas_call(
        body,
        grid=(batch, nqb),
        in_specs=[
            pl.BlockSpec((Q_BLOCK, nh * hd), lambda b, q: (b * nqb + q, Z_Q // (nh * hd))),
            pl.BlockSpec((Q_BLOCK, V7X_LANES), lambda b, q: (b * nqb + q, Z_G // V7X_LANES)),
            pl.BlockSpec((1, nc, hd), lambda b, q: (b, 0, 0)),
            pl.BlockSpec((1, nc, hd), lambda b, q: (b, 0, 0)),
            pl.BlockSpec((seq, 2 * hd), lambda b, q: (b, Z_KV // (2 * hd) + 1)),
            pl.BlockSpec((1,) + win_pad.shape[1:], lambda b, q: (b, 0, 0)),
            pl.BlockSpec(bank.shape, lambda b, q: (0, 0, 0)),
            pl.BlockSpec((nh, Q_BLOCK, nc), lambda b, q: (0, q, 0)),
            pl.BlockSpec(cover.shape, lambda b, q: (0, 0)),
            pl.BlockSpec(expand.shape, lambda b, q: (0, 0)),
        ],
        out_specs=pl.BlockSpec((Q_BLOCK, nh * hd), lambda b, q: (b * nqb + q, 0)),
        out_shape=jax.ShapeDtypeStruct((batch * seq, nh * hd), F32),
        scratch_shapes=[
            pltpu.VMEM((rows, hd), BF16),
            pltpu.VMEM((seq // tk, Q_BLOCK, tk), F32),
            pltpu.VMEM((rows, hd), F32),
            pltpu.VMEM((nh, Q_BLOCK, 1), F32),
            pltpu.VMEM((nh, Q_BLOCK, 1), F32),
            pltpu.VMEM((nh, Q_BLOCK, hd), F32),
            pltpu.VMEM((nh, Q_BLOCK, hd), F32),
        ],
        compiler_params=_cparams(("parallel", "arbitrary")),
        name="nsa_prompt",
    )(z, z, cmp_k, cmp_v, z, win_pad, bank, bank_c, cover, expand)


def _nsa_sample_body(pt_ref, q_ref, g_ref, new_ref, ck_ref, cv_ref, win_ref, cache_ref,
                     bankc_ref, bankw_ref, banks_ref, bankt_ref, cover_ref, expand_ref,
                     o_ref,
                     buf, sem, qs_s, wkv_s, selx_s, m_s, l_s, acc_s,
                     *, n_seq, n_pages, page, chunk, n_slc_blocks):
    s_id = pl.program_id(0)
    slot = s_id % 2
    ds = q_ref.shape[0]
    hd = NSA_HEAD_DIM
    nh = NSA_HEADS
    rows = nh * ds
    past = n_pages * page
    n_chunks = past // chunk
    scale = hd ** -0.5

    @pl.when(s_id == 0)
    def _first():
        _page_gather(pt_ref, cache_ref, buf, sem, 0, 0, n_pages, 0, page, True)

    @pl.when(s_id + 1 < n_seq)
    def _prefetch():
        _page_gather(pt_ref, cache_ref, buf, sem, s_id + 1, 0, n_pages, (s_id + 1) % 2, page, True)

    new_kv = new_ref[...]
    tail = buf.shape[1] - past
    buf[slot, past:past + ds, :] = new_kv[:, 2 * hd:4 * hd]
    buf[slot, past + ds:, :] = jnp.zeros((tail - ds, 2 * hd), F32)

    qs = jnp.concatenate(
        [(q_ref[:, h * hd:(h + 1) * hd] * scale).astype(BF16) for h in range(nh)], axis=0)
    qs_s[...] = qs
    t_col = lax.broadcasted_iota(I32, (rows, 1), 0) & (ds - 1)

    nc = ck_ref.shape[1]
    s_c = _dot_nt(qs, ck_ref[0].astype(BF16)) + bankc_ref[...]
    n_col = lax.broadcasted_iota(I32, (rows, nc), 1)
    valid_c = (n_col * CMP_STRIDE + (CMP_BLOCK - 1)) <= (past + t_col)
    p_c = _softmax_rows(s_c, valid_c)
    o_c = _dot(p_c.astype(BF16), cv_ref[0].astype(BF16))

    ph = p_c[0:ds, :]
    for h in range(1, nh):
        ph = ph + p_c[h * ds:(h + 1) * ds, :]
    hi, lo = _split_bf16(ph)
    imp = _dot(hi, cover_ref[...]) + _dot(lo, cover_ref[...])
    ns_pad = imp.shape[1]
    blk = lax.broadcasted_iota(I32, (ds, ns_pad), 1)
    qpos = past + lax.broadcasted_iota(I32, (ds, ns_pad), 0)
    cur = lax.shift_right_logical(qpos, int(math.log2(SLC_BLOCK)))
    forced = (blk == 0) | (blk == cur) | (blk == cur - 1)
    visible = (blk * SLC_BLOCK) <= qpos
    score = jnp.where(forced, FORCED_SCORE, jnp.where(visible, imp, -1.0))
    score = jnp.where(blk < n_slc_blocks, score, PAD_SCORE)
    score_t = jnp.concatenate([score, jnp.zeros((V7X_LANES - ds, ns_pad), F32)], axis=0).T
    sp_idx = lax.broadcasted_iota(I32, (ns_pad, ns_pad), 0)
    s_idx = lax.broadcasted_iota(I32, (ns_pad, ns_pad), 1)
    ranks = []
    for t in range(ds):
        c = score_t[:, t:t + 1]
        r = score[t:t + 1, :]
        ahead = (c > r) | ((c == r) & (sp_idx < s_idx))
        ranks.append(jnp.sum(jnp.where(ahead, 1.0, 0.0), axis=0, keepdims=True))
    rank = jnp.concatenate(ranks, axis=0)
    sel = jnp.where((rank < float(min(N_SELECT, n_slc_blocks))) & (blk < n_slc_blocks), 1.0, 0.0)

    bpc = chunk // SLC_BLOCK
    n_groups = selx_s.shape[0] // ds
    lane = lax.broadcasted_iota(I32, (ds, V7X_LANES), 1)
    pieces = []
    for gi in range(n_groups):
        a, off = (gi * bpc) // V7X_LANES, (gi * bpc) % V7X_LANES
        tile = sel[:, a * V7X_LANES:(a + 1) * V7X_LANES]
        if off:
            tile = pltpu.roll(tile, V7X_LANES - off, 1)
        pieces.append(jnp.where(lane < bpc, tile, 0.0))
    selx_s[...] = _dot(jnp.concatenate(pieces, axis=0).astype(BF16), expand_ref[...])

    _page_gather(pt_ref, cache_ref, buf, sem, s_id, 0, n_pages, slot, page, False)
    _flash_reset(m_s, l_s, acc_s)
    far_bias = banks_ref[:, 0:1].reshape(nh, ds, 1)

    def slc_chunk(c, bias):
        k0 = pl.multiple_of(c * chunk, chunk)
        kv = buf[slot, pl.ds(k0, chunk), :]
        s = _dot_nt(qs_s[...], kv[:, 0:hd].astype(BF16)).reshape(nh, ds, chunk) + bias
        valid = selx_s[pl.ds(pl.multiple_of(c * ds, ds), ds), :] > 0.5
        _flash_update(s, valid[None], kv[:, hd:2 * hd].astype(BF16), m_s, l_s, acc_s)

    def slc_step(c, carry):
        slc_chunk(c, far_bias)
        return carry

    lax.fori_loop(0, n_chunks - 1, slc_step, 0)
    slc_chunk(n_chunks - 1, banks_ref[...].reshape(nh, ds, chunk))

    kv = buf[slot, past:past + tail, :]
    s_t = _dot_nt(qs, kv[:, 0:hd].astype(BF16)) + bankt_ref[...]
    j_t = lax.broadcasted_iota(I32, (ds, tail), 1)
    t_t = lax.broadcasted_iota(I32, (ds, tail), 0)
    valid_t = (selx_s[n_chunks * ds:(n_chunks + 1) * ds, 0:tail] > 0.5) & (j_t <= t_t)
    _flash_update(s_t.reshape(nh, ds, tail), valid_t[None], kv[:, hd:2 * hd].astype(BF16),
                  m_s, l_s, acc_s)
    o_s = acc_s[...] / l_s[...]

    wb = win_ref.shape[1]
    nw = wkv_s.shape[0]
    wkv_s[0:wb, :] = win_ref[0]
    wkv_s[wb:wb + ds, :] = new_kv[:, 4 * hd:6 * hd]
    wkv_s[wb + ds:, :] = jnp.zeros((nw - wb - ds, 2 * hd), F32)
    wkv = wkv_s[...]
    s_w = _dot_nt(qs, wkv[:, 0:hd].astype(BF16)) + bankw_ref[...]
    w_col = lax.broadcasted_iota(I32, (rows, nw), 1)
    valid_w = ((w_col - wb) <= t_col) & ((w_col - wb) > (t_col - WINDOW)) & (w_col < wb + ds)
    p_w = _softmax_rows(s_w, valid_w)
    o_w = _dot(p_w.astype(BF16), wkv[:, hd:2 * hd].astype(BF16))

    gate = jax.nn.sigmoid(g_ref[...])
    for h in range(nh):
        o_ref[:, h * hd:(h + 1) * hd] = (
            gate[:, 3 * h:3 * h + 1] * o_c[h * ds:(h + 1) * ds, :]
            + gate[:, 3 * h + 1:3 * h + 2] * o_s[h]
            + gate[:, 3 * h + 2:3 * h + 3] * o_w[h * ds:(h + 1) * ds, :])


def _nsa_sample(page_table, z, cmp_k, cmp_v, win_buf, cache, bank_c, bank_w, bank_s, bank_t,
                cover, expand, *, row0, ds, chunk=2048):
    n_seq, n_pages = page_table.shape
    page = cache.shape[1]
    hd = NSA_HEAD_DIM
    nh = NSA_HEADS
    rows = nh * ds
    past = n_pages * page
    assert past % chunk == 0 and row0 % ds == 0 and chunk // SLC_BLOCK <= V7X_LANES
    n_slc_blocks = -(-(past + ds) // SLC_BLOCK)
    tail = V7X_LANES
    n_groups = past // chunk + 1
    nc = cmp_k.shape[1]
    wb = win_buf.shape[1]
    nw = bank_w.shape[1]
    r0 = row0 // ds
    body = functools.partial(_nsa_sample_body, n_seq=n_seq, n_pages=n_pages, page=page,
                             chunk=chunk, n_slc_blocks=n_slc_blocks)
    const2 = lambda s, pt: (0, 0)
    grid_spec = pltpu.PrefetchScalarGridSpec(
        num_scalar_prefetch=1,
        grid=(n_seq,),
        in_specs=[
            pl.BlockSpec((ds, nh * hd), lambda s, pt: (r0 + s, Z_Q // (nh * hd))),
            pl.BlockSpec((ds, V7X_LANES), lambda s, pt: (r0 + s, Z_G // V7X_LANES)),
            pl.BlockSpec((ds, 6 * hd), lambda s, pt: (r0 + s, Z_KV // (6 * hd))),
            pl.BlockSpec((1, nc, hd), lambda s, pt: (s, 0, 0)),
            pl.BlockSpec((1, nc, hd), lambda s, pt: (s, 0, 0)),
            pl.BlockSpec((1, wb, 2 * hd), lambda s, pt: (s, 0, 0)),
            pl.BlockSpec(memory_space=pl.ANY),
            pl.BlockSpec(bank_c.shape, const2),
            pl.BlockSpec(bank_w.shape, const2),
            pl.BlockSpec(bank_s.shape, const2),
            pl.BlockSpec(bank_t.shape, const2),
            pl.BlockSpec(cover.shape, const2),
            pl.BlockSpec(expand.shape, const2),
        ],
        out_specs=pl.BlockSpec((ds, nh * hd), lambda s, pt: (s, 0)),
        scratch_shapes=[
            pltpu.VMEM((2, past + tail, 2 * hd), F32),
            pltpu.SemaphoreType.DMA((2,)),
            pltpu.VMEM((rows, hd), BF16),
            pltpu.VMEM((nw, 2 * hd), F32),
            pltpu.VMEM((n_groups * ds, chunk), F32),
            pltpu.VMEM((nh, ds, 1), F32),
            pltpu.VMEM((nh, ds, 1), F32),
            pltpu.VMEM((nh, ds, hd), F32),
        ],
    )
    return pl.pallas_call(
        body,
        grid_spec=grid_spec,
        out_shape=jax.ShapeDtypeStruct((n_seq * ds, nh * hd), F32),
        compiler_params=_cparams(("arbitrary",)),
        name="nsa_sample",
    )(page_table, z, z, z, cmp_k, cmp_v, win_buf, cache, bank_c, bank_w, bank_s, bank_t,
      cover, expand)


def _conv_silu(g, g1, g2, u, cw_ref, cb_ref, col0, n_valid):
    cw = cw_ref[...]
    conv = cb_ref[...] + cw[0:1] * g2 + cw[1:2] * g1 + cw[2:3] * g
    act = jax.nn.silu(conv) * u
    col = col0 + lax.broadcasted_iota(I32, act.shape, act.ndim - 1)
    return jnp.where(col < n_valid, act, 0.0)


def _ffn_up_prompt_body(a_ref, wg_ref, wu_ref, cw_ref, cb_ref, act_ref, tail_ref,
                        wg_s, wu_s, carry_s, *, tiles_per_seq, n_valid):
    n = pl.program_id(0)
    m = pl.program_id(1)
    tm, tn = act_ref.shape

    @pl.when(m == 0)
    def _cast():
        wg_s[...] = wg_ref[...].astype(BF16)
        wu_s[...] = wu_ref[...].astype(BF16)

    @pl.when(m % tiles_per_seq == 0)
    def _seq_start():
        carry_s[...] = jnp.zeros_like(carry_s)

    a = a_ref[...]
    g = _dot(a, wg_s[...])
    u = _dot(a, wu_s[...])
    prev = carry_s[...]
    row = lax.broadcasted_iota(I32, (tm, tn), 0)
    last = V7X_SUBLANES - 1
    g1 = jnp.where(row == 0, prev[last:last + 1], pltpu.roll(g, 1, 0))
    g2 = jnp.where(row == 0, prev[last - 1:last],
                   jnp.where(row == 1, prev[last:last + 1], pltpu.roll(g, 2, 0)))
    act_ref[...] = _conv_silu(g, g1, g2, u, cw_ref, cb_ref, n * tn, n_valid).astype(act_ref.dtype)
    carry_s[...] = g[tm - V7X_SUBLANES:tm, :]
    tail_ref[0] = g[tm - V7X_SUBLANES:tm, :]


def _ffn_up_prompt(hn, w_gate, w_up, conv_w, conv_b, *, rows, seq, tm=1024, tn=256):
    d, dff = w_gate.shape
    n_blocks = pl.cdiv(dff, tn)
    assert rows % tm == 0 and seq % tm == 0
    body = functools.partial(_ffn_up_prompt_body, tiles_per_seq=seq // tm, n_valid=dff)
    return pl.pallas_call(
        body,
        grid=(n_blocks, rows // tm),
        in_specs=[
            pl.BlockSpec((tm, d), lambda n, m: (m, 0)),
            pl.BlockSpec((d, tn), lambda n, m: (0, n)),
            pl.BlockSpec((d, tn), lambda n, m: (0, n)),
            pl.BlockSpec((conv_w.shape[0], tn), lambda n, m: (0, n)),
            pl.BlockSpec((1, tn), lambda n, m: (0, n)),
        ],
        out_specs=[
            pl.BlockSpec((tm, tn), lambda n, m: (m, n)),
            pl.BlockSpec((1, V7X_SUBLANES, tn), lambda n, m: (m, 0, n)),
        ],
        out_shape=[
            jax.ShapeDtypeStruct((rows, dff), BF16),
            jax.ShapeDtypeStruct((rows // tm, V7X_SUBLANES, dff), F32),
        ],
        scratch_shapes=[pltpu.VMEM((d, tn), BF16), pltpu.VMEM((d, tn), BF16),
                        pltpu.VMEM((V7X_SUBLANES, tn), F32)],
        compiler_params=_cparams(("arbitrary", "arbitrary")),
        name="ffn_up_prompt",
    )(hn, w_gate, w_up, conv_w, conv_b.reshape(1, dff))


def _ffn_up_sample_body(a_ref, wg_ref, wu_ref, cw_ref, cb_ref, st_ref, act_ref, st_out_ref,
                        *, ds, n_valid):
    n = pl.program_id(0)
    tm, tn = act_ref.shape
    nseq = tm // ds
    a = a_ref[...]
    g = _dot(a, wg_ref[...].astype(BF16)).reshape(nseq, ds, tn)
    u = _dot(a, wu_ref[...].astype(BF16)).reshape(nseq, ds, tn)
    st = st_ref[...]
    s0, s1 = st[:, 0:1, :], st[:, 1:2, :]
    t = lax.broadcasted_iota(I32, (nseq, ds, tn), 1)
    g1 = jnp.where(t == 0, s1, pltpu.roll(g, 1, 1))
    g2 = jnp.where(t == 0, s0, jnp.where(t == 1, s1, pltpu.roll(g, 2, 1)))
    cw = cw_ref[...]
    conv = cb_ref[...] + cw[0:1] * g2 + cw[1:2] * g1 + cw[2:3] * g
    act = jax.nn.silu(conv) * u
    col = n * tn + lax.broadcasted_iota(I32, act.shape, 2)
    act = jnp.where(col < n_valid, act, 0.0)
    act_ref[...] = act.reshape(tm, tn).astype(act_ref.dtype)
    st_out_ref[...] = g[:, ds - 2:ds, :]


def _ffn_up_sample(hn, w_gate, w_up, conv_w, conv_b, state, *, row0, rows, ds, tn=256):
    d, dff = w_gate.shape
    n_blocks = pl.cdiv(dff, tn)
    assert row0 % rows == 0 and rows % ds == 0 and ds == V7X_SUBLANES
    nseq = rows // ds
    r0 = row0 // rows
    body = functools.partial(_ffn_up_sample_body, ds=ds, n_valid=dff)
    return pl.pallas_call(
        body,
        grid=(n_blocks,),
        in_specs=[
            pl.BlockSpec((rows, d), lambda n: (r0, 0)),
            pl.BlockSpec((d, tn), lambda n: (0, n)),
            pl.BlockSpec((d, tn), lambda n: (0, n)),
            pl.BlockSpec((conv_w.shape[0], tn), lambda n: (0, n)),
            pl.BlockSpec((1, tn), lambda n: (0, n)),
            pl.BlockSpec((nseq, state.shape[1], tn), lambda n: (0, 0, n)),
        ],
        out_specs=[
            pl.BlockSpec((rows, tn), lambda n: (0, n)),
            pl.BlockSpec((nseq, state.shape[1], tn), lambda n: (0, 0, n)),
        ],
        out_shape=[
            jax.ShapeDtypeStruct((rows, dff), BF16),
            jax.ShapeDtypeStruct(state.shape, F32),
        ],
        compiler_params=_cparams(("parallel",)),
        name="ffn_up_sample",
    )(hn, w_gate, w_up, conv_w, conv_b.reshape(1, dff), state)


def _layout_w_in(w_in):
    d = w_in.shape[0]
    nq = NSA_HEADS * NSA_HEAD_DIM
    nkv = 6 * NSA_HEAD_DIM
    ng = 3 * NSA_HEADS
    o_kv, o_g = nq, nq + nkv
    o_qa = o_g + ng
    o_kva = o_qa + MLA_Q_LORA
    o_m = o_kva + MLA_KV_LORA + MLA_ROPE
    d_model = (w_in.shape[1] - o_m) // 2

    def zeros(n):
        return jnp.zeros((d, n), w_in.dtype)

    parts = [
        w_in[:, 0:nq],
        w_in[:, o_g:o_g + ng], zeros(Z_QA - Z_G - ng),
        w_in[:, o_qa:o_qa + MLA_Q_LORA],
        w_in[:, o_kv:o_kv + nkv],
        w_in[:, o_kva + MLA_KV_LORA:o_kva + MLA_KV_LORA + MLA_ROPE],
        zeros(Z_CKV - Z_KPE - MLA_ROPE),
        w_in[:, o_kva:o_kva + MLA_KV_LORA],
        w_in[:, o_m:o_m + 2 * d_model],
    ]
    out = jnp.concatenate(parts, axis=1).astype(BF16)
    assert out.shape[1] == Z_WIDTH and Z_MB - Z_MA == d_model
    return out


def _rope_tables(pos):
    half = MLA_ROPE // 2
    inv = ROPE_THETA ** (-jnp.arange(half, dtype=F32) / half)
    ang = pos.astype(F32)[:, None] * inv
    cos, sin = jnp.cos(ang), jnp.sin(ang)
    pad = jnp.zeros((pos.shape[0], V7X_LANES - MLA_ROPE), F32)
    return (jnp.concatenate([cos, cos, pad], axis=1), jnp.concatenate([-sin, sin, pad], axis=1))


def kernel(x_prompt, x_sample, cache_mla, cache_nsa_cmp_kv, cache_nsa_slc_kv, cache_nsa_win_kv,
           state_ffn_conv, page_table, rel_bias, norm_attn, w_in, nsa_phi_pos, nsa_phi_w1, nsa_phi_w2,
           mla_q_norm, mla_w_uq, mla_kv_norm, mla_w_ukv, w_br_nsa, w_br_mla, w_out, norm_ffn,
           w_ffn_gate, w_ffn_up, ffn_conv_w, ffn_conv_b, w_ffn_down, norm_final):
    depth = norm_attn.shape[0]
    assert depth == 1, "the layer loop is written for the single-layer configuration"
    lyr = 0
    b, seq, d = x_prompt.shape
    db, ds, _ = x_sample.shape
    mp, msm = b * seq, db * ds
    m = mp + msm
    n_pool, page = cache_mla.shape[1], cache_mla.shape[2]
    n_pages = page_table.shape[1]
    past = n_pages * page
    hd = NSA_HEAD_DIM
    nh = NSA_HEADS
    assert seq % Q_BLOCK == 0 and ds == V7X_SUBLANES and past >= WINDOW
    assert cache_nsa_win_kv.shape[2] == WINDOW and seq >= WINDOW

    x_all = jnp.concatenate([x_prompt.reshape(mp, d), x_sample.reshape(msm, d)], axis=0)
    xn = _rmsnorm(x_all, norm_attn[lyr], BF16, name="norm_attn")
    z = _matmul(xn, _layout_w_in(w_in[lyr]), tm=1024, tn=1280, tk=2048, out_dtype=F32, name="in_proj")

    pos_all = jnp.concatenate([jnp.tile(jnp.arange(seq, dtype=I32), b),
                               jnp.tile(past + jnp.arange(ds, dtype=I32), db)])
    cs, sn = _rope_tables(pos_all)
    w_uq = mla_w_uq[lyr].reshape(MLA_Q_LORA, MLA_HEADS, MLA_NOPE + MLA_ROPE)
    w_uq_p = jnp.concatenate([
        w_uq[:, :, :MLA_NOPE].reshape(MLA_Q_LORA, MLA_HEADS * MLA_NOPE),
        jnp.pad(w_uq[:, :, MLA_NOPE:], ((0, 0), (0, 0), (0, V7X_LANES - MLA_ROPE))).reshape(
            MLA_Q_LORA, MLA_HEADS * V7X_LANES)], axis=1).astype(BF16)
    w_ukv = mla_w_ukv[lyr]
    wk = jnp.transpose(w_ukv[:, :, :MLA_NOPE], (1, 2, 0)).astype(BF16)
    w_uv = jnp.transpose(w_ukv[:, :, MLA_NOPE:], (1, 0, 2)).astype(BF16)
    q_lat, q_pe, lat = _mla_project(z, cs, sn, mla_q_norm[lyr], w_uq_p, mla_kv_norm[lyr], wk)

    pt = page_table + lyr * n_pool
    lat_w = MLA_KV_LORA + MLA_ROPE
    o_lat_p = _mla_prompt(q_lat, q_pe, lat, batch=b, seq=seq)
    o_lat_s = _mla_sample(pt, q_lat, q_pe, lat, cache_mla.reshape(depth * n_pool, page, lat_w),
                          row0=mp, ds=ds)
    o_mla = jnp.concatenate([_mla_uv(o_lat_p, w_uv, name="mla_uv_prompt"),
                             _mla_uv(o_lat_s, w_uv, name="mla_uv_sample")], axis=0)

    pad = WINDOW + Q_BLOCK
    i_q = np.arange(Q_BLOCK)[:, None]
    bank = _bias_bank(rel_bias, _rel_bucket_np(i_q + pad - np.arange(8 * V7X_LANES)[None, :]), "bias_bank_prompt")
    bank = jnp.transpose(bank.reshape(nh, Q_BLOCK, 8, V7X_LANES), (2, 0, 1, 3)).reshape(
        8, nh * Q_BLOCK, V7X_LANES)
    nc_p = seq // CMP_STRIDE
    qpos_p = np.arange(seq)[:, None]
    bank_cp = _bias_bank(rel_bias, _rel_bucket_np(
        qpos_p - (np.arange(nc_p)[None, :] * CMP_STRIDE + CMP_BLOCK - 1)), "bias_bank_prompt_cmp")
    ns_p = -(-seq // SLC_BLOCK)

    def cover_matrix(n_cmp, n_cmp_pad, n_slc, n_slc_pad):
        c_start = np.arange(n_cmp_pad)[:, None] * CMP_STRIDE
        s_start = np.arange(n_slc_pad)[None, :] * SLC_BLOCK
        cov = (c_start < s_start + SLC_BLOCK) & (c_start + CMP_BLOCK > s_start)
        cov &= (np.arange(n_cmp_pad)[:, None] < n_cmp) & (np.arange(n_slc_pad)[None, :] < n_slc)
        return jnp.asarray(cov.astype(np.float32), dtype=BF16)

    cover_p = cover_matrix(seq // CMP_STRIDE - 1, nc_p, ns_p, V7X_LANES)
    expand_p = jnp.asarray(
        (np.arange(seq)[None, :] // SLC_BLOCK == np.arange(V7X_LANES)[:, None]).astype(np.float32), dtype=BF16)
    phi = nsa_phi_pos[lyr].reshape(CMP_BLOCK, 2 * hd)
    w1 = nsa_phi_w1[lyr].astype(BF16)
    w2 = nsa_phi_w2[lyr]
    cmp_k, cmp_v = _cmp_prompt(z, phi, w1, w2, batch=b, seq=seq)
    win_new = z[:, Z_KV + 4 * hd:Z_KV + 6 * hd]
    win_pad = jnp.pad(win_new[:mp].reshape(b, seq, 2 * hd), ((0, 0), (pad, 0), (0, 0)))
    o_nsa_p = _nsa_prompt(z, cmp_k, cmp_v, win_pad, bank, bank_cp, cover_p, expand_p, batch=b, seq=seq)

    cache_cmp = cache_nsa_cmp_kv.reshape(depth * n_pool, page, 2 * hd)
    cache_slc = cache_nsa_slc_kv.reshape(depth * n_pool, page, 2 * hd)
    cmp_ks, cmp_vs = _cmp_sample(pt, z, cache_cmp, phi, w1, w2, row0=mp, ds=ds)
    nc_s = cmp_ks.shape[1]
    chunk = 2048
    t_q = np.arange(ds)[:, None]
    wb = cache_nsa_win_kv.shape[2]
    nw = -(-(wb + ds) // V7X_LANES) * V7X_LANES

    def sample_bank(dist, name):
        return _bias_bank(rel_bias, _rel_bucket_np(dist), name).reshape(nh * ds, dist.shape[1])

    bank_cs = sample_bank(past + t_q - (np.arange(nc_s)[None, :] * CMP_STRIDE + CMP_BLOCK - 1), "bias_bank_sample_cmp")
    bank_ws = sample_bank(t_q + wb - np.arange(nw)[None, :], "bias_bank_sample_win")
    bank_ss = sample_bank(past + t_q - (past - chunk + np.arange(chunk)[None, :]), "bias_bank_sample_slc")
    bank_ts = sample_bank(t_q - np.arange(V7X_LANES)[None, :], "bias_bank_sample_new")
    ns_s = -(-(past + ds) // SLC_BLOCK)
    ns_s_pad = -(-ns_s // V7X_LANES) * V7X_LANES
    n_cmp_s = -(-(past + ds) // CMP_STRIDE) - CMP_BLOCK // CMP_STRIDE + 1
    cover_s = cover_matrix(n_cmp_s, nc_s, ns_s, ns_s_pad)
    expand_s = jnp.asarray(
        (np.arange(chunk)[None, :] // SLC_BLOCK == np.arange(V7X_LANES)[:, None]).astype(np.float32), dtype=BF16)
    o_nsa_s = _nsa_sample(pt, z, cmp_ks, cmp_vs, cache_nsa_win_kv[lyr].reshape(db, wb, 2 * hd), cache_slc,
                          bank_cs, bank_ws, bank_ss, bank_ts, cover_s, expand_s,
                          row0=mp, ds=ds, chunk=chunk)
    o_nsa = jnp.concatenate([o_nsa_p, o_nsa_s], axis=0)

    t1 = _matmul(o_nsa, w_br_nsa[lyr], tm=1024, tn=512, tk=2048, out_dtype=F32, extras=[(z, Z_MA)],
                 epilogue=lambda acc, zm: jax.nn.sigmoid(zm) * acc, name="merge_nsa")
    y = _matmul(o_mla, w_br_mla[lyr], tm=1024, tn=512, tk=2048, out_dtype=BF16,
                extras=[(z, Z_MB), (t1, 0)],
                epilogue=lambda acc, zm, t: t + jax.nn.sigmoid(zm) * acc, name="merge_mla")
    h1 = _matmul(y, w_out[lyr], tm=1024, tn=1024, tk=2048, out_dtype=F32, extras=[(x_all, 0)],
                 epilogue=lambda acc, x: x + acc, name="out_proj")

    hn = _rmsnorm(h1, norm_ffn[lyr], BF16, name="norm_ffn")
    dff = w_ffn_gate.shape[2]
    assert ffn_conv_w.shape[1] == 3 and state_ffn_conv.shape[2] == 2
    tm_f = 1024
    act_p, tails = _ffn_up_prompt(hn, w_ffn_gate[lyr], w_ffn_up[lyr], ffn_conv_w[lyr], ffn_conv_b[lyr],
                                  rows=mp, seq=seq, tm=tm_f)
    act_s, conv_s = _ffn_up_sample(hn, w_ffn_gate[lyr], w_ffn_up[lyr], ffn_conv_w[lyr], ffn_conv_b[lyr],
                                   state_ffn_conv[lyr], row0=mp, rows=msm, ds=ds)
    add = lambda acc, h: h + acc
    h2_p = _matmul(act_p, w_ffn_down[lyr], tm=1024, tn=1024, tk=1024, out_dtype=F32, extras=[(h1, 0)],
                   epilogue=add, name="ffn_down_prompt")
    h2_s = _matmul(act_s, w_ffn_down[lyr], tm=1024, tn=1024, tk=1024, out_dtype=F32, extras=[(h1, 0, mp)],
                   epilogue=add, name="ffn_down_sample")
    y_p = _rmsnorm(h2_p, norm_final, F32, name="norm_final_prompt").reshape(b, seq, d)
    y_s = _rmsnorm(h2_s, norm_final, F32, name="norm_final_sample").reshape(db, ds, d)

    kv_tail = (2, 1, hd)
    kv_new = z[:, Z_KV:Z_KV + 6 * hd]
    kv_p = kv_new[:mp].reshape(b, seq, 3, *kv_tail)
    kv_s = kv_new[mp:].reshape(db, ds, 3, *kv_tail)
    win_all = jnp.concatenate([cache_nsa_win_kv[lyr], kv_s[:, :, 2]], axis=1)
    tps = seq // tm_f
    conv_p = tails.reshape(b, tps, V7X_SUBLANES, dff)[:, tps - 1, V7X_SUBLANES - 2:, :]
    return (y_p, y_s,
            lat[:mp].reshape(1, b, seq, lat_w), lat[mp:].reshape(1, db, ds, lat_w),
            kv_p[None, :, :, 0], kv_s[None, :, :, 0],
            kv_p[None, :, :, 1], kv_s[None, :, :, 1],
            kv_p[None, :, seq - min(WINDOW, seq):, 2],
            win_all[None, :, wb + ds - min(WINDOW, wb + ds):],
            conv_p[None], conv_s[None])
```

```python
import functools
import math

import numpy as np
import jax
import jax.numpy as jnp
from jax import lax
from jax.experimental import pallas as pl
from jax.experimental.pallas import tpu as pltpu

F32, BF16, I32 = jnp.float32, jnp.bfloat16, jnp.int32

V7X_LANES = 128
V7X_SUBLANES = 8
V7X_VMEM_BYTES = 64 * 1024 * 1024
VMEM_LIMIT = V7X_VMEM_BYTES - 8 * 1024 * 1024

NSA_HEADS = 16
NSA_HEAD_DIM = 128
CMP_BLOCK = 32
CMP_STRIDE = 16
SLC_BLOCK = 64
N_SELECT = 16
WINDOW = 512
MLA_HEADS = 16
MLA_Q_LORA = 768
MLA_KV_LORA = 512
MLA_NOPE = 128
MLA_ROPE = 64
MLA_V = 128
MLA_SCALE = (MLA_NOPE + MLA_ROPE) ** -0.5
ROPE_THETA = 10000.0
REL_BUCKETS = 32
REL_MAX_DIST = 128
RMS_EPS = 1e-6
Q_BLOCK = 128
NEG_INF = -1e30
FORCED_SCORE = 1e30
PAD_SCORE = -3e38

Z_Q = 0
Z_G = 2048
Z_QA = 2304
Z_KV = 3072
Z_KPE = 3840
Z_CKV = 4096
Z_MA = 4608
Z_MB = 8704
Z_WIDTH = 12800


def _cparams(sem, vmem=VMEM_LIMIT):
    return pltpu.CompilerParams(dimension_semantics=sem, vmem_limit_bytes=vmem)


def _dot(a, b):
    return jnp.dot(a, b, preferred_element_type=F32)


def _dot_nt(a, b):
    return lax.dot_general(a, b, (((1,), (1,)), ((), ())), preferred_element_type=F32)


def _split_bf16(x):
    hi = x.astype(BF16)
    lo = (x - hi.astype(F32)).astype(BF16)
    return hi, lo


def _rmsnorm_body(x_ref, g_ref, o_ref):
    x = x_ref[...]
    ms = jnp.mean(x * x, axis=-1, keepdims=True)
    o_ref[...] = (x * lax.rsqrt(ms + RMS_EPS) * g_ref[...]).astype(o_ref.dtype)


def _rmsnorm(x, g, out_dtype, *, row0=0, rows=None, tm=256, name="rmsnorm"):
    m_all, d = x.shape
    rows = m_all if rows is None else rows
    assert rows % tm == 0 and row0 % tm == 0
    b0 = row0 // tm
    return pl.pallas_call(
        _rmsnorm_body,
        grid=(rows // tm,),
        in_specs=[pl.BlockSpec((tm, d), lambda i: (b0 + i, 0)),
                  pl.BlockSpec((1, d), lambda i: (0, 0))],
        out_specs=pl.BlockSpec((tm, d), lambda i: (i, 0)),
        out_shape=jax.ShapeDtypeStruct((rows, d), out_dtype),
        compiler_params=_cparams(("parallel",)),
        name=name,
    )(x, g.reshape(1, d))


def _mm_body(*refs, nk, tk, k_valid, n_extra, epilogue):
    a_ref, w_ref = refs[0], refs[1]
    extra = refs[2:2 + n_extra]
    o_ref, acc_ref = refs[2 + n_extra], refs[3 + n_extra]
    k = pl.program_id(2)

    @pl.when(k == 0)
    def _init():
        acc_ref[...] = jnp.zeros_like(acc_ref)

    a = a_ref[...].astype(BF16)
    w = w_ref[...].astype(BF16)
    if k_valid % tk:
        row = lax.broadcasted_iota(I32, w.shape, 0) + k * tk
        w = jnp.where(row < k_valid, w, jnp.zeros_like(w))
        col = lax.broadcasted_iota(I32, a.shape, 1) + k * tk
        a = jnp.where(col < k_valid, a, jnp.zeros_like(a))
    acc_ref[...] += _dot(a, w)

    @pl.when(k == nk - 1)
    def _fin():
        o_ref[...] = epilogue(acc_ref[...], *(e[...] for e in extra)).astype(o_ref.dtype)


def _matmul(a, w, *, tm, tn, tk, out_dtype, a_col0=0, extras=(), epilogue=None, name):
    m = a.shape[0]
    k_valid, n = w.shape
    nk = pl.cdiv(k_valid, tk)
    assert m % tm == 0 and n % tn == 0 and a_col0 % tk == 0
    assert a.shape[1] >= a_col0 + k_valid
    if epilogue is None:
        epilogue = lambda acc: acc
    ka0 = a_col0 // tk
    in_specs = [pl.BlockSpec((tm, tk), lambda i, j, k: (i, ka0 + k)),
                pl.BlockSpec((tk, tn), lambda i, j, k: (k, j))]
    args = [a, w]
    for arr, col0, *rest in extras:
        row0 = rest[0] if rest else 0
        assert col0 % tn == 0 and row0 % tm == 0
        in_specs.append(pl.BlockSpec((tm, tn), lambda i, j, k, c=col0 // tn, r=row0 // tm: (r + i, c + j)))
        args.append(arr)
    body = functools.partial(_mm_body, nk=nk, tk=tk, k_valid=k_valid, n_extra=len(extras),
                             epilogue=epilogue)
    return pl.pallas_call(
        body,
        grid=(m // tm, n // tn, nk),
        in_specs=in_specs,
        out_specs=pl.BlockSpec((tm, tn), lambda i, j, k: (i, j)),
        out_shape=jax.ShapeDtypeStruct((m, n), out_dtype),
        scratch_shapes=[pltpu.VMEM((tm, tn), F32)],
        compiler_params=_cparams(("parallel", "parallel", "arbitrary")),
        name=name,
    )(*args)


def _rope_tile(x, cs, sn):
    lane = lax.broadcasted_iota(I32, x.shape, x.ndim - 1) & (V7X_LANES - 1)
    half = MLA_ROPE // 2
    rot = jnp.where(lane < half, pltpu.roll(x, V7X_LANES - half, x.ndim - 1),
                    pltpu.roll(x, half, x.ndim - 1))
    return x * cs + rot * sn


def _mla_proj_body(qa_ref, kpe_ref, ckv_ref, cs_ref, sn_ref, qn_ref, wuq_ref, kvn_ref, wk_ref,
                   qlat_ref, qpe_ref, lat_ref):
    qa = qa_ref[...]
    ms = jnp.mean(qa * qa, axis=-1, keepdims=True)
    qan = (qa * lax.rsqrt(ms + RMS_EPS) * qn_ref[...]).astype(BF16)
    q = _dot(qan, wuq_ref[...])
    nope_w = MLA_HEADS * MLA_NOPE
    for h in range(MLA_HEADS):
        qn_h = q[:, h * MLA_NOPE:(h + 1) * MLA_NOPE].astype(BF16)
        qlat_ref[:, h * MLA_KV_LORA:(h + 1) * MLA_KV_LORA] = _dot(qn_h, wk_ref[h])
    cs = cs_ref[...]
    sn = sn_ref[...]
    for h in range(MLA_HEADS):
        sl = slice(h * V7X_LANES, (h + 1) * V7X_LANES)
        qpe_ref[:, sl] = _rope_tile(q[:, nope_w + h * V7X_LANES:nope_w + (h + 1) * V7X_LANES], cs, sn)
    ckv = ckv_ref[...]
    ms2 = jnp.mean(ckv * ckv, axis=-1, keepdims=True)
    lat_ref[:, 0:MLA_KV_LORA] = ckv * lax.rsqrt(ms2 + RMS_EPS) * kvn_ref[...]
    kpe = _rope_tile(kpe_ref[...], cs, sn)
    lat_ref[:, MLA_KV_LORA:MLA_KV_LORA + MLA_ROPE] = kpe[:, 0:MLA_ROPE]


def _mla_project(z, cs, sn, q_norm, w_uq_p, kv_norm, wk, *, tm=256):
    m = z.shape[0]
    lat_w = MLA_KV_LORA + MLA_ROPE
    return pl.pallas_call(
        _mla_proj_body,
        grid=(m // tm,),
        in_specs=[
            pl.BlockSpec((tm, MLA_Q_LORA), lambda i: (i, Z_QA // MLA_Q_LORA)),
            pl.BlockSpec((tm, V7X_LANES), lambda i: (i, Z_KPE // V7X_LANES)),
            pl.BlockSpec((tm, MLA_KV_LORA), lambda i: (i, Z_CKV // MLA_KV_LORA)),
            pl.BlockSpec((tm, V7X_LANES), lambda i: (i, 0)),
            pl.BlockSpec((tm, V7X_LANES), lambda i: (i, 0)),
            pl.BlockSpec((1, MLA_Q_LORA), lambda i: (0, 0)),
            pl.BlockSpec(w_uq_p.shape, lambda i: (0, 0)),
            pl.BlockSpec((1, MLA_KV_LORA), lambda i: (0, 0)),
            pl.BlockSpec(wk.shape, lambda i: (0, 0, 0)),
        ],
        out_specs=[
            pl.BlockSpec((tm, MLA_HEADS * MLA_KV_LORA), lambda i: (i, 0)),
            pl.BlockSpec((tm, MLA_HEADS * V7X_LANES), lambda i: (i, 0)),
            pl.BlockSpec((tm, lat_w), lambda i: (i, 0)),
        ],
        out_shape=[
            jax.ShapeDtypeStruct((m, MLA_HEADS * MLA_KV_LORA), F32),
            jax.ShapeDtypeStruct((m, MLA_HEADS * V7X_LANES), F32),
            jax.ShapeDtypeStruct((m, lat_w), F32),
        ],
        compiler_params=_cparams(("parallel",)),
        name="mla_project",
    )(z, z, z, cs, sn, q_norm.reshape(1, -1), w_uq_p, kv_norm.reshape(1, -1), wk)


def _stack_heads(ref, width, valid=None):
    valid = width if valid is None else valid
    return jnp.concatenate(
        [ref[:, h * width:h * width + valid].astype(BF16) for h in range(MLA_HEADS)], axis=0)


def _mla_prompt_body(qlat_ref, qpe_ref, lat_ref, o_ref, qlat_s, qpe_s, m_s, l_s, acc_s, *, tk):
    qb = pl.program_id(1)
    nq = qlat_ref.shape[0]
    rows = MLA_HEADS * nq
    qlat_s[...] = _stack_heads(qlat_ref, MLA_KV_LORA)
    qpe_s[...] = _stack_heads(qpe_ref, V7X_LANES, MLA_ROPE)
    m_s[...] = jnp.full(m_s.shape, NEG_INF, F32)
    l_s[...] = jnp.zeros(l_s.shape, F32)
    acc_s[...] = jnp.zeros(acc_s.shape, F32)
    qpos = qb * nq + (lax.broadcasted_iota(I32, (rows, tk), 0) & (nq - 1))
    lane = lax.broadcasted_iota(I32, (rows, tk), 1)

    def step(kt, masked):
        k0 = pl.multiple_of(kt * tk, tk)
        kv = lat_ref[pl.ds(k0, tk), :]
        ckv = kv[:, 0:MLA_KV_LORA].astype(BF16)
        kpe = kv[:, MLA_KV_LORA:MLA_KV_LORA + MLA_ROPE].astype(BF16)
        s = (_dot_nt(qlat_s[...], ckv) + _dot_nt(qpe_s[...], kpe)) * MLA_SCALE
        if masked:
            valid = (k0 + lane) <= qpos
            s = jnp.where(valid, s, NEG_INF)
        m_prev = m_s[...]
        m_new = jnp.maximum(m_prev, jnp.max(s, axis=1, keepdims=True))
        alpha = jnp.exp(m_prev - m_new)
        p = jnp.exp(s - m_new)
        if masked:
            p = jnp.where(valid, p, 0.0)
        l_s[...] = alpha * l_s[...] + jnp.sum(p, axis=1, keepdims=True)
        acc_s[...] = alpha * acc_s[...] + _dot(p.astype(BF16), ckv)
        m_s[...] = m_new

    def full_step(kt, carry):
        step(kt, False)
        return carry

    def diag_step(kt, carry):
        step(kt, True)
        return carry

    n_full = (qb * nq) // tk
    n_tiles = ((qb + 1) * nq + tk - 1) // tk
    lax.fori_loop(0, n_full, full_step, 0)
    lax.fori_loop(n_full, n_tiles, diag_step, 0)
    o = acc_s[...] / l_s[...]
    for h in range(MLA_HEADS):
        o_ref[:, h * MLA_KV_LORA:(h + 1) * MLA_KV_LORA] = o[h * nq:(h + 1) * nq, :]


def _mla_prompt(q_lat, q_pe, lat, *, batch, seq, tk=256):
    nqb = seq // Q_BLOCK
    rows = MLA_HEADS * Q_BLOCK
    lat_w = MLA_KV_LORA + MLA_ROPE
    return pl.pallas_call(
        functools.partial(_mla_prompt_body, tk=tk),
        grid=(batch, nqb),
        in_specs=[
            pl.BlockSpec((Q_BLOCK, MLA_HEADS * MLA_KV_LORA), lambda b, q: (b * nqb + q, 0)),
            pl.BlockSpec((Q_BLOCK, MLA_HEADS * V7X_LANES), lambda b, q: (b * nqb + q, 0)),
            pl.BlockSpec((seq, lat_w), lambda b, q: (b, 0)),
        ],
        out_specs=pl.BlockSpec((Q_BLOCK, MLA_HEADS * MLA_KV_LORA), lambda b, q: (b * nqb + q, 0)),
        out_shape=jax.ShapeDtypeStruct((batch * seq, MLA_HEADS * MLA_KV_LORA), F32),
        scratch_shapes=[
            pltpu.VMEM((rows, MLA_KV_LORA), BF16),
            pltpu.VMEM((rows, MLA_ROPE), BF16),
            pltpu.VMEM((rows, 1), F32),
            pltpu.VMEM((rows, 1), F32),
            pltpu.VMEM((rows, MLA_KV_LORA), F32),
        ],
        compiler_params=_cparams(("parallel", "arbitrary")),
        name="mla_prompt",
    )(q_lat, q_pe, lat)


def _page_gather(pt_ref, cache_ref, dst_of, sem, seq, page0, n_pages, start, unroll=1):
    def body(g, carry):
        for u in range(unroll):
            src = cache_ref.at[pt_ref[seq, page0 + g * unroll + u]] if start else cache_ref.at[0]
            cp = pltpu.make_async_copy(src, dst_of(g, u), sem)
            if start:
                cp.start()
            else:
                cp.wait()
        return carry
    lax.fori_loop(0, n_pages // unroll, body, 0)


def _row_pages(buf, slot, rows_per_page):
    return lambda g, u: buf.at[slot, pl.ds((g + u) * rows_per_page, rows_per_page)]


def _mla_sample_body(pt_ref, qlat_ref, qpe_ref, new_ref, cache_ref, o_ref,
                     buf, sem, qlat_s, qpe_s, new_s, m_s, l_s, acc_s,
                     *, n_seq, n_chunks, ppc, page, sub):
    s_id = pl.program_id(0)
    c_id = pl.program_id(1)
    g = s_id * n_chunks + c_id
    slot = g % 2
    ds = qlat_ref.shape[0]
    rows = MLA_HEADS * ds
    pps = sub // page

    def dst_of(slot_):
        return lambda gi, u: buf.at[slot_, gi, :, u * page:(u + 1) * page]

    @pl.when(g == 0)
    def _first():
        _page_gather(pt_ref, cache_ref, dst_of(0), sem.at[0], 0, 0, ppc, True, unroll=pps)

    @pl.when(g + 1 < n_seq * n_chunks)
    def _prefetch():
        g1 = g + 1
        _page_gather(pt_ref, cache_ref, dst_of(g1 % 2), sem.at[g1 % 2], g1 // n_chunks,
                     (g1 % n_chunks) * ppc, ppc, True, unroll=pps)

    @pl.when(c_id == 0)
    def _init():
        qlat_s[...] = _stack_heads(qlat_ref, MLA_KV_LORA)
        qpe_s[...] = _stack_heads(qpe_ref, V7X_LANES, MLA_ROPE)
        m_s[...] = jnp.full(m_s.shape, NEG_INF, F32)
        l_s[...] = jnp.zeros(l_s.shape, F32)
        acc_s[...] = jnp.zeros(acc_s.shape, F32)

    _page_gather(pt_ref, cache_ref, dst_of(slot), sem.at[slot], s_id, 0, ppc, False, unroll=pps)

    def flash(s, valid, pv_of):
        s = s * MLA_SCALE
        if valid is not None:
            s = jnp.where(valid, s, NEG_INF)
        m_prev = m_s[...]
        m_new = jnp.maximum(m_prev, jnp.max(s, axis=1, keepdims=True))
        alpha = jnp.exp(m_prev - m_new)
        p = jnp.exp(s - m_new)
        if valid is not None:
            p = jnp.where(valid, p, 0.0)
        l_s[...] = alpha * l_s[...] + jnp.sum(p, axis=1, keepdims=True)
        acc_s[...] = alpha * acc_s[...] + pv_of(p.astype(BF16))
        m_s[...] = m_new

    def step(j, carry):
        kt = buf[slot, j]
        ckv_t = kt[0:MLA_KV_LORA, :].astype(BF16)
        kpe_t = kt[MLA_KV_LORA:MLA_KV_LORA + MLA_ROPE, :].astype(BF16)
        s = _dot(qlat_s[...], ckv_t) + _dot(qpe_s[...], kpe_t)
        flash(s, None, lambda p: _dot_nt(p, ckv_t))
        return carry

    lax.fori_loop(0, (ppc * page) // sub, step, 0)

    @pl.when(c_id == n_chunks - 1)
    def _finish():
        new_s[...] = jnp.zeros(new_s.shape, F32)
        new_s[0:ds, :] = new_ref[...]
        nk = new_s.shape[0]
        t_q = lax.broadcasted_iota(I32, (rows, nk), 0) & (ds - 1)
        j_k = lax.broadcasted_iota(I32, (rows, nk), 1)
        kv = new_s[...]
        ckv = kv[:, 0:MLA_KV_LORA].astype(BF16)
        kpe = kv[:, MLA_KV_LORA:MLA_KV_LORA + MLA_ROPE].astype(BF16)
        s = _dot_nt(qlat_s[...], ckv) + _dot_nt(qpe_s[...], kpe)
        flash(s, j_k <= t_q, lambda p: _dot(p, ckv))
        o = acc_s[...] / l_s[...]
        for h in range(MLA_HEADS):
            o_ref[:, h * MLA_KV_LORA:(h + 1) * MLA_KV_LORA] = o[h * ds:(h + 1) * ds, :]


def _mla_sample(page_table, q_lat, q_pe, lat, cache_t, *, row0, ds, ppc=32, sub=1024):
    n_seq, n_pages = page_table.shape
    lat_w, page = cache_t.shape[1], cache_t.shape[2]
    assert n_pages % ppc == 0 and (ppc * page) % sub == 0 and sub % page == 0 and row0 % ds == 0
    n_chunks = n_pages // ppc
    rows = MLA_HEADS * ds
    r0 = row0 // ds
    body = functools.partial(_mla_sample_body, n_seq=n_seq, n_chunks=n_chunks, ppc=ppc, page=page,
                             sub=sub)
    grid_spec = pltpu.PrefetchScalarGridSpec(
        num_scalar_prefetch=1,
        grid=(n_seq, n_chunks),
        in_specs=[
            pl.BlockSpec((ds, MLA_HEADS * MLA_KV_LORA), lambda s, c, pt: (r0 + s, 0)),
            pl.BlockSpec((ds, MLA_HEADS * V7X_LANES), lambda s, c, pt: (r0 + s, 0)),
            pl.BlockSpec((ds, lat_w), lambda s, c, pt: (r0 + s, 0)),
            pl.BlockSpec(memory_space=pl.ANY),
        ],
        out_specs=pl.BlockSpec((ds, MLA_HEADS * MLA_KV_LORA), lambda s, c, pt: (s, 0)),
        scratch_shapes=[
            pltpu.VMEM((2, (ppc * page) // sub, lat_w, sub), F32),
            pltpu.SemaphoreType.DMA((2,)),
            pltpu.VMEM((rows, MLA_KV_LORA), BF16),
            pltpu.VMEM((rows, MLA_ROPE), BF16),
            pltpu.VMEM((V7X_LANES, lat_w), F32),
            pltpu.VMEM((rows, 1), F32),
            pltpu.VMEM((rows, 1), F32),
            pltpu.VMEM((rows, MLA_KV_LORA), F32),
        ],
    )
    return pl.pallas_call(
        body,
        grid_spec=grid_spec,
        out_shape=jax.ShapeDtypeStruct((n_seq * ds, MLA_HEADS * MLA_KV_LORA), F32),
        compiler_params=_cparams(("arbitrary", "arbitrary")),
        name="mla_sample",
    )(page_table, q_lat, q_pe, lat, cache_t)


def _mla_uv_body(o_ref, w_ref, out_ref):
    for h in range(MLA_HEADS):
        o_h = o_ref[:, h * MLA_KV_LORA:(h + 1) * MLA_KV_LORA].astype(BF16)
        out_ref[:, h * MLA_V:(h + 1) * MLA_V] = _dot(o_h, w_ref[h]).astype(out_ref.dtype)


def _mla_uv(o_lat, w_uv, *, tm=256, name="mla_uv"):
    m = o_lat.shape[0]
    assert m % tm == 0
    return pl.pallas_call(
        _mla_uv_body,
        grid=(m // tm,),
        in_specs=[pl.BlockSpec((tm, MLA_HEADS * MLA_KV_LORA), lambda i: (i, 0)),
                  pl.BlockSpec(w_uv.shape, lambda i: (0, 0, 0))],
        out_specs=pl.BlockSpec((tm, MLA_HEADS * MLA_V), lambda i: (i, 0)),
        out_shape=jax.ShapeDtypeStruct((m, MLA_HEADS * MLA_V), BF16),
        compiler_params=_cparams(("parallel",)),
        name=name,
    )(o_lat, w_uv)


def _rel_bucket_np(dist):
    n = np.maximum(dist, 0)
    max_exact = REL_BUCKETS // 2
    nf = np.maximum(n, max_exact).astype(np.float32)
    large = max_exact + (np.log(nf / max_exact) / math.log(REL_MAX_DIST / max_exact)
                         * (REL_BUCKETS - max_exact)).astype(np.int32)
    large = np.minimum(large, REL_BUCKETS - 1)
    return np.where(n < max_exact, n, large).astype(np.int32)


def _bias_bank_body(rb_ref, bucket_ref, o_ref):
    h = pl.program_id(0)
    bucket = bucket_ref[...]
    acc = jnp.zeros(bucket.shape, F32)
    for b in range(REL_BUCKETS):
        acc = jnp.where(bucket == b, rb_ref[b, h], acc)
    o_ref[0] = acc


def _bias_bank(rel_bias, bucket, name):
    r, c = bucket.shape
    return pl.pallas_call(
        _bias_bank_body,
        grid=(NSA_HEADS,),
        in_specs=[pl.BlockSpec(memory_space=pltpu.SMEM),
                  pl.BlockSpec((r, c), lambda h: (0, 0))],
        out_specs=pl.BlockSpec((1, r, c), lambda h: (h, 0, 0)),
        out_shape=jax.ShapeDtypeStruct((NSA_HEADS, r, c), F32),
        compiler_params=_cparams(("parallel",)),
        name=name,
    )(rel_bias, jnp.asarray(bucket))


def _gelu_tanh(x):
    return 0.5 * x * (1.0 + jnp.tanh(math.sqrt(2.0 / math.pi) * (x + 0.044715 * x * x * x)))


def _compress_rows(load_rows, n_blocks, phi_ref, w1_ref, w2_ref, flat_s, hi_s, outs):
    hd = NSA_HEAD_DIM
    gc = flat_s.shape[0]
    gb = gc - V7X_SUBLANES
    nh1 = w1_ref.shape[2] // 2
    for c in range(2):
        phi_c = phi_ref[:, c * hd:(c + 1) * hd]
        phi_flat = jnp.concatenate(
            [jnp.concatenate([phi_c[half * CMP_STRIDE + i:half * CMP_STRIDE + i + 1, :]
                              for i in range(CMP_STRIDE)], axis=1) for half in range(2)]
            + [jnp.zeros((V7X_SUBLANES - 2, CMP_STRIDE * hd), F32)], axis=0)
        pw = _dot(phi_flat.astype(BF16), w1_ref[c])
        const = pw[0:1, 0:nh1] + pw[1:2, nh1:2 * nh1]
        for g0 in range(0, n_blocks, gb):
            for i in range(CMP_STRIDE):
                x = load_rows(c, g0 * CMP_STRIDE + i, gc, CMP_STRIDE)
                flat_s[:, i * hd:(i + 1) * hd] = x.astype(BF16)
            ab = _dot(flat_s[...], w1_ref[c])
            hi_s[...] = ab[:, nh1:2 * nh1]
            hid = _gelu_tanh(ab[0:gb, 0:nh1] + hi_s[1:gb + 1, :] + const)
            outs[c][0, g0:g0 + gb, :] = _dot(hid.astype(BF16), w2_ref[c].astype(BF16))


def _cmp_prompt_body(x_ref, phi_ref, w1_ref, w2_ref, ck_ref, cv_ref, xpad_s, flat_s, hi_s, *, n_blocks):
    seq = x_ref.shape[0]
    hd = NSA_HEAD_DIM
    for c in range(2):
        xpad_s[c, 0:seq, :] = x_ref[:, c * hd:(c + 1) * hd]
        xpad_s[c, seq:, :] = jnp.zeros((xpad_s.shape[1] - seq, hd), F32)
    load = lambda c, first, count, stride: xpad_s[c, pl.ds(first, count, stride=stride), :]
    _compress_rows(load, n_blocks, phi_ref, w1_ref, w2_ref, flat_s, hi_s, (ck_ref, cv_ref))


def _cmp_prompt(z, phi, w1, w2, *, batch, seq):
    hd = NSA_HEAD_DIM
    n_blocks = seq // CMP_STRIDE
    n_chunks = n_blocks + V7X_SUBLANES
    pad_rows = n_chunks * CMP_STRIDE
    out = jax.ShapeDtypeStruct((batch, n_blocks, hd), F32)
    return pl.pallas_call(
        functools.partial(_cmp_prompt_body, n_blocks=n_blocks),
        grid=(batch,),
        in_specs=[
            pl.BlockSpec((seq, 2 * hd), lambda b: (b, Z_KV // (2 * hd))),
            pl.BlockSpec(phi.shape, lambda b: (0, 0)),
            pl.BlockSpec(w1.shape, lambda b: (0, 0, 0)),
            pl.BlockSpec(w2.shape, lambda b: (0, 0, 0)),
        ],
        out_specs=[pl.BlockSpec((1, n_blocks, hd), lambda b: (b, 0, 0))] * 2,
        out_shape=[out, out],
        scratch_shapes=[pltpu.VMEM((2, pad_rows, hd), F32),
                        pltpu.VMEM((n_chunks, CMP_STRIDE * hd), BF16),
                        pltpu.VMEM((n_chunks, w1.shape[2] // 2), F32)],
        compiler_params=_cparams(("parallel",)),
        name="nsa_cmp_prompt",
    )(z, phi, w1, w2)


def _cmp_sample_body(pt_ref, new_ref, cache_ref, phi_ref, w1_ref, w2_ref, ck_ref, cv_ref,
                     buf, sem, flat_s, hi_s, *, n_seq, n_pages, page, n_blocks):
    s_id = pl.program_id(0)
    slot = s_id % 2
    past = n_pages * page
    ds = new_ref.shape[0]

    hd = NSA_HEAD_DIM

    @pl.when(s_id == 0)
    def _first():
        _page_gather(pt_ref, cache_ref, _row_pages(buf, 0, 2 * page), sem.at[0], 0, 0, n_pages, True)

    @pl.when(s_id + 1 < n_seq)
    def _prefetch():
        nxt = (s_id + 1) % 2
        _page_gather(pt_ref, cache_ref, _row_pages(buf, nxt, 2 * page), sem.at[nxt], s_id + 1, 0,
                     n_pages, True)

    _store_kv_rows(buf, slot, past, new_ref[:, 0:2 * hd])
    buf[slot, 2 * (past + ds):, :] = jnp.zeros((buf.shape[1] - 2 * (past + ds), hd), F32)
    _page_gather(pt_ref, cache_ref, _row_pages(buf, slot, 2 * page), sem.at[slot], s_id, 0, n_pages, False)
    load = lambda c, first, count, stride: buf[slot, pl.ds(2 * first + c, count, stride=2 * stride), :]
    _compress_rows(load, n_blocks, phi_ref, w1_ref, w2_ref, flat_s, hi_s, (ck_ref, cv_ref))


def _store_kv_rows(buf, slot, first_token, kv):
    hd = NSA_HEAD_DIM
    for c in range(2):
        buf[slot, pl.ds(2 * first_token + c, kv.shape[0], stride=2), :] = kv[:, c * hd:(c + 1) * hd]


def _cmp_sample(page_table, z, cache, phi, w1, w2, *, row0, ds):
    n_seq, n_pages = page_table.shape
    page = cache.shape[1] // 2
    hd = NSA_HEAD_DIM
    past = n_pages * page
    n_chunk = -(-(past + ds) // CMP_STRIDE)
    n_blocks = n_chunk - CMP_BLOCK // CMP_STRIDE + 1
    gb = min(n_blocks, 512)
    assert n_blocks % gb == 0 and gb % V7X_SUBLANES == 0 and row0 % ds == 0
    gc = gb + V7X_SUBLANES
    buf_rows = (n_blocks + V7X_SUBLANES) * CMP_STRIDE
    r0 = row0 // ds
    out = jax.ShapeDtypeStruct((n_seq, n_blocks, hd), F32)
    grid_spec = pltpu.PrefetchScalarGridSpec(
        num_scalar_prefetch=1,
        grid=(n_seq,),
        in_specs=[
            pl.BlockSpec((ds, 2 * hd), lambda s, pt: (r0 + s, Z_KV // (2 * hd))),
            pl.BlockSpec(memory_space=pl.ANY),
            pl.BlockSpec(phi.shape, lambda s, pt: (0, 0)),
            pl.BlockSpec(w1.shape, lambda s, pt: (0, 0, 0)),
            pl.BlockSpec(w2.shape, lambda s, pt: (0, 0, 0)),
        ],
        out_specs=[pl.BlockSpec((1, n_blocks, hd), lambda s, pt: (s, 0, 0))] * 2,
        scratch_shapes=[
            pltpu.VMEM((2, 2 * buf_rows, hd), F32),
            pltpu.SemaphoreType.DMA((2,)),
            pltpu.VMEM((gc, CMP_STRIDE * hd), BF16),
            pltpu.VMEM((gc, w1.shape[2] // 2), F32),
        ],
    )
    body = functools.partial(_cmp_sample_body, n_seq=n_seq, n_pages=n_pages, page=page,
                             n_blocks=n_blocks)
    return pl.pallas_call(
        body,
        grid_spec=grid_spec,
        out_shape=[out, out],
        compiler_params=_cparams(("arbitrary",)),
        name="nsa_cmp_sample",
    )(page_table, z, cache, phi, w1, w2)


def _softmax_rows(s, valid):
    s = jnp.where(valid, s, NEG_INF)
    m = jnp.max(s, axis=-1, keepdims=True)
    p = jnp.where(valid, jnp.exp(s - m), 0.0)
    l = jnp.sum(p, axis=-1, keepdims=True)
    return p / jnp.where(l > 0.0, l, 1.0)


def _flash_update(s, valid, v, m_s, l_s, acc_s):
    nh, nq, tk = s.shape
    s = jnp.where(valid, s, NEG_INF)
    m_prev = m_s[...]
    m_new = jnp.maximum(m_prev, jnp.max(s, axis=-1, keepdims=True))
    alpha = jnp.exp(m_prev - m_new)
    p = jnp.where(valid, jnp.exp(s - m_new), 0.0)
    l_s[...] = alpha * l_s[...] + jnp.sum(p, axis=-1, keepdims=True)
    pv = _dot(p.astype(BF16).reshape(nh * nq, tk), v)
    acc_s[...] = alpha * acc_s[...] + pv.reshape(nh, nq, v.shape[1])
    m_s[...] = m_new


def _flash_reset(m_s, l_s, acc_s):
    m_s[...] = jnp.full(m_s.shape, NEG_INF, F32)
    l_s[...] = jnp.zeros(l_s.shape, F32)
    acc_s[...] = jnp.zeros(acc_s.shape, F32)


def _nsa_prompt_body(q_ref, g_ref, ck_ref, cv_ref, slc_ref, win_ref, bank_ref, bankc_ref,
                     cover_ref, expand_ref, o_ref,
                     qs_s, sel_s, oc_s, m_s, l_s, acc_s, os_s, *, tk, n_slc_blocks):
    qb = pl.program_id(1)
    nq = q_ref.shape[0]
    hd = NSA_HEAD_DIM
    nh = NSA_HEADS
    rows = nh * nq
    start = qb * nq
    scale = hd ** -0.5
    qs_s[...] = jnp.concatenate(
        [(q_ref[:, h * hd:(h + 1) * hd] * scale).astype(BF16) for h in range(nh)], axis=0)

    nc = ck_ref.shape[1]
    s_c = _dot_nt(qs_s[...], ck_ref[0].astype(BF16)) + bankc_ref[...].reshape(rows, nc)
    i_row = lax.broadcasted_iota(I32, (rows, nc), 0) & (nq - 1)
    n_col = lax.broadcasted_iota(I32, (rows, nc), 1)
    valid_c = (n_col * CMP_STRIDE + (CMP_BLOCK - 1)) <= (start + i_row)
    p_c = _softmax_rows(s_c, valid_c)
    oc_s[...] = _dot(p_c.astype(BF16), cv_ref[0].astype(BF16))

    ph = p_c[0:nq, :]
    for h in range(1, nh):
        ph = ph + p_c[h * nq:(h + 1) * nq, :]
    hi, lo = _split_bf16(ph)
    imp = _dot(hi, cover_ref[...]) + _dot(lo, cover_ref[...])
    ns_pad = imp.shape[1]
    blk = lax.broadcasted_iota(I32, (nq, ns_pad), 1)
    qpos = start + lax.broadcasted_iota(I32, (nq, ns_pad), 0)
    cur = lax.shift_right_logical(qpos, int(math.log2(SLC_BLOCK)))
    forced = (blk == 0) | (blk == cur) | (blk == cur - 1)
    visible = (blk * SLC_BLOCK) <= qpos
    score = jnp.where(forced, FORCED_SCORE, jnp.where(visible, imp, -1.0))
    score = jnp.where(blk < n_slc_blocks, score, PAD_SCORE)
    rank = jnp.zeros((nq, ns_pad), F32)
    for sp in range(n_slc_blocks):
        col = score[:, sp:sp + 1]
        ahead = (col > score) | ((col == score) & (blk > sp))
        rank = rank + jnp.where(ahead, 1.0, 0.0)
    sel = jnp.where((rank < float(min(N_SELECT, n_slc_blocks))) & (blk < n_slc_blocks), 1.0, 0.0)
    selx = _dot(sel.astype(BF16), expand_ref[...])
    n_kt = selx.shape[1] // tk
    for kt in range(n_kt):
        sel_s[kt] = selx[:, kt * tk:(kt + 1) * tk]

    i3 = lax.broadcasted_iota(I32, (nq, tk), 0)
    j3 = lax.broadcasted_iota(I32, (nq, tk), 1)
    n_bank = bank_ref.shape[0]
    half = tk // V7X_LANES

    def bias_tile(r0_tiles):
        parts = [bank_ref[jnp.clip(r0_tiles + u, 0, n_bank - 1)] for u in range(half)]
        return jnp.concatenate(parts, axis=1)

    _flash_reset(m_s, l_s, acc_s)
    assert nq == V7X_LANES
    org_tiles = qb - (WINDOW + Q_BLOCK) // V7X_LANES

    def slc_step(kt, carry):
        k0 = pl.multiple_of(kt * tk, tk)
        kv = slc_ref[pl.ds(k0, tk), :]
        s = _dot_nt(qs_s[...], kv[:, 0:hd].astype(BF16))
        s = s + bias_tile(kt * half - org_tiles)
        valid = (sel_s[kt] > 0.5) & ((k0 + j3) <= (start + i3))
        s3 = s.reshape(nh, nq, tk)
        _flash_update(s3, valid[None], kv[:, hd:2 * hd].astype(BF16), m_s, l_s, acc_s)
        return carry

    lax.fori_loop(0, (start + nq + tk - 1) // tk, slc_step, 0)
    os_s[...] = acc_s[...] / l_s[...]

    _flash_reset(m_s, l_s, acc_s)
    pad = WINDOW + Q_BLOCK
    for d in range(-(-(pad + nq) // tk)):
        r0 = d * tk
        kv = win_ref[0, pl.ds(start + r0, tk), :]
        kpos = start - pad + r0 + j3
        s = _dot_nt(qs_s[...], kv[:, 0:hd].astype(BF16)) + bias_tile(d * half)
        qp = start + i3
        valid = (kpos <= qp) & (kpos > qp - WINDOW) & (kpos >= 0)
        _flash_update(s.reshape(nh, nq, tk), valid[None], kv[:, hd:2 * hd].astype(BF16),
                      m_s, l_s, acc_s)
    o_w = acc_s[...] / l_s[...]

    gate = jax.nn.sigmoid(g_ref[...])
    for h in range(nh):
        o_ref[:, h * hd:(h + 1) * hd] = (
            gate[:, 3 * h:3 * h + 1] * oc_s[h * nq:(h + 1) * nq, :]
            + gate[:, 3 * h + 1:3 * h + 2] * os_s[h]
            + gate[:, 3 * h + 2:3 * h + 3] * o_w[h])


def _nsa_prompt(z, cmp_k, cmp_v, win_pad, bank, bank_c, cover, expand, *, batch, seq, tk=256):
    nqb = seq // Q_BLOCK
    hd = NSA_HEAD_DIM
    nh = NSA_HEADS
    rows = nh * Q_BLOCK
    nc = cmp_k.shape[1]
    n_slc_blocks = -(-seq // SLC_BLOCK)
    body = functools.partial(_nsa_prompt_body, tk=tk, n_slc_blocks=n_slc_blocks)
    return pl.pallas_call(
        body,
        grid=(batch, nqb),
        in_specs=[
            pl.BlockSpec((Q_BLOCK, nh * hd), lambda b, q: (b * nqb + q, Z_Q // (nh * hd))),
            pl.BlockSpec((Q_BLOCK, V7X_LANES), lambda b, q: (b * nqb + q, Z_G // V7X_LANES)),
            pl.BlockSpec((1, nc, hd), lambda b, q: (b, 0, 0)),
            pl.BlockSpec((1, nc, hd), lambda b, q: (b, 0, 0)),
            pl.BlockSpec((seq, 2 * hd), lambda b, q: (b, Z_KV // (2 * hd) + 1)),
            pl.BlockSpec((1,) + win_pad.shape[1:], lambda b, q: (b, 0, 0)),
            pl.BlockSpec(bank.shape, lambda b, q: (0, 0, 0)),
            pl.BlockSpec((nh, Q_BLOCK, nc), lambda b, q: (0, q, 0)),
            pl.BlockSpec(cover.shape, lambda b, q: (0, 0)),
            pl.BlockSpec(expand.shape, lambda b, q: (0, 0)),
        ],
        out_specs=pl.BlockSpec((Q_BLOCK, nh * hd), lambda b, q: (b * nqb + q, 0)),
        out_shape=jax.ShapeDtypeStruct((batch * seq, nh * hd), F32),
        scratch_shapes=[
            pltpu.VMEM((rows, hd), BF16),
            pltpu.VMEM((seq // tk, Q_BLOCK, tk), F32),
            pltpu.VMEM((rows, hd), F32),
            pltpu.VMEM((nh, Q_BLOCK, 1), F32),
            pltpu.VMEM((nh, Q_BLOCK, 1), F32),
            pltpu.VMEM((nh, Q_BLOCK, hd), F32),
            pltpu.VMEM((nh, Q_BLOCK, hd), F32),
        ],
        compiler_params=_cparams(("parallel", "arbitrary")),
        name="nsa_prompt",
    )(z, z, cmp_k, cmp_v, z, win_pad, bank, bank_c, cover, expand)


def _nsa_sample_body(pt_ref, q_ref, g_ref, new_ref, ck_ref, cv_ref, win_ref, cache_ref,
                     bankc_ref, bankw_ref, banks_ref, bankt_ref, cover_ref, expand_ref,
                     o_ref,
                     buf, sem, qs_s, wkv_s, selx_s, m_s, l_s, acc_s,
                     *, n_seq, n_pages, page, chunk, n_slc_blocks):
    s_id = pl.program_id(0)
    slot = s_id % 2
    ds = q_ref.shape[0]
    hd = NSA_HEAD_DIM
    nh = NSA_HEADS
    rows = nh * ds
    past = n_pages * page
    n_chunks = past // chunk
    scale = hd ** -0.5

    @pl.when(s_id == 0)
    def _first():
        _page_gather(pt_ref, cache_ref, _row_pages(buf, 0, 2 * page), sem.at[0], 0, 0, n_pages, True)

    @pl.when(s_id + 1 < n_seq)
    def _prefetch():
        nxt = (s_id + 1) % 2
        _page_gather(pt_ref, cache_ref, _row_pages(buf, nxt, 2 * page), sem.at[nxt], s_id + 1, 0,
                     n_pages, True)

    new_kv = new_ref[...]
    tail = buf.shape[1] // 2 - past
    _store_kv_rows(buf, slot, past, new_kv[:, 2 * hd:4 * hd])
    buf[slot, 2 * (past + ds):, :] = jnp.zeros((2 * (tail - ds), hd), F32)

    def load_kv(first, count):
        return (buf[slot, pl.ds(2 * first, count, stride=2), :].astype(BF16),
                buf[slot, pl.ds(2 * first + 1, count, stride=2), :].astype(BF16))

    qs = jnp.concatenate(
        [(q_ref[:, h * hd:(h + 1) * hd] * scale).astype(BF16) for h in range(nh)], axis=0)
    qs_s[...] = qs
    t_col = lax.broadcasted_iota(I32, (rows, 1), 0) & (ds - 1)

    nc = ck_ref.shape[1]
    s_c = _dot_nt(qs, ck_ref[0].astype(BF16)) + bankc_ref[...]
    n_col = lax.broadcasted_iota(I32, (rows, nc), 1)
    valid_c = (n_col * CMP_STRIDE + (CMP_BLOCK - 1)) <= (past + t_col)
    p_c = _softmax_rows(s_c, valid_c)
    o_c = _dot(p_c.astype(BF16), cv_ref[0].astype(BF16))

    ph = p_c[0:ds, :]
    for h in range(1, nh):
        ph = ph + p_c[h * ds:(h + 1) * ds, :]
    hi, lo = _split_bf16(ph)
    imp = _dot(hi, cover_ref[...]) + _dot(lo, cover_ref[...])
    ns_pad = imp.shape[1]
    blk = lax.broadcasted_iota(I32, (ds, ns_pad), 1)
    qpos = past + lax.broadcasted_iota(I32, (ds, ns_pad), 0)
    cur = lax.shift_right_logical(qpos, int(math.log2(SLC_BLOCK)))
    forced = (blk == 0) | (blk == cur) | (blk == cur - 1)
    visible = (blk * SLC_BLOCK) <= qpos
    score = jnp.where(forced, FORCED_SCORE, jnp.where(visible, imp, -1.0))
    score = jnp.where(blk < n_slc_blocks, score, PAD_SCORE)
    score_t = jnp.concatenate([score, jnp.zeros((V7X_LANES - ds, ns_pad), F32)], axis=0).T
    sp_idx = lax.broadcasted_iota(I32, (ns_pad, ns_pad), 0)
    s_idx = lax.broadcasted_iota(I32, (ns_pad, ns_pad), 1)
    ranks = []
    for t in range(ds):
        c = score_t[:, t:t + 1]
        r = score[t:t + 1, :]
        ahead = (c > r) | ((c == r) & (sp_idx < s_idx))
        ranks.append(jnp.sum(jnp.where(ahead, 1.0, 0.0), axis=0, keepdims=True))
    rank = jnp.concatenate(ranks, axis=0)
    sel = jnp.where((rank < float(min(N_SELECT, n_slc_blocks))) & (blk < n_slc_blocks), 1.0, 0.0)

    bpc = chunk // SLC_BLOCK
    n_groups = selx_s.shape[0] // ds
    lane = lax.broadcasted_iota(I32, (ds, V7X_LANES), 1)
    pieces = []
    for gi in range(n_groups):
        a, off = (gi * bpc) // V7X_LANES, (gi * bpc) % V7X_LANES
        tile = sel[:, a * V7X_LANES:(a + 1) * V7X_LANES]
        if off:
            tile = pltpu.roll(tile, V7X_LANES - off, 1)
        pieces.append(jnp.where(lane < bpc, tile, 0.0))
    selx_s[...] = _dot(jnp.concatenate(pieces, axis=0).astype(BF16), expand_ref[...])

    _page_gather(pt_ref, cache_ref, _row_pages(buf, slot, 2 * page), sem.at[slot], s_id, 0, n_pages, False)
    _flash_reset(m_s, l_s, acc_s)
    far_bias = banks_ref[:, 0:1].reshape(nh, ds, 1)

    def slc_chunk(c, bias):
        k, v = load_kv(pl.multiple_of(c * chunk, chunk), chunk)
        s = _dot_nt(qs_s[...], k).reshape(nh, ds, chunk) + bias
        valid = selx_s[pl.ds(pl.multiple_of(c * ds, ds), ds), :] > 0.5
        _flash_update(s, valid[None], v, m_s, l_s, acc_s)

    def slc_step(c, carry):
        slc_chunk(c, far_bias)
        return carry

    lax.fori_loop(0, n_chunks - 1, slc_step, 0)
    slc_chunk(n_chunks - 1, banks_ref[...].reshape(nh, ds, chunk))

    k_t, v_t = load_kv(past, tail)
    s_t = _dot_nt(qs, k_t) + bankt_ref[...]
    j_t = lax.broadcasted_iota(I32, (ds, tail), 1)
    t_t = lax.broadcasted_iota(I32, (ds, tail), 0)
    valid_t = (selx_s[n_chunks * ds:(n_chunks + 1) * ds, 0:tail] > 0.5) & (j_t <= t_t)
    _flash_update(s_t.reshape(nh, ds, tail), valid_t[None], v_t, m_s, l_s, acc_s)
    o_s = acc_s[...] / l_s[...]

    wb = win_ref.shape[1] // 2
    nw = wkv_s.shape[1]
    for c in range(2):
        wkv_s[c, 0:wb, :] = win_ref[0, pl.ds(c, wb, stride=2), :]
        wkv_s[c, wb:wb + ds, :] = new_kv[:, (4 + c) * hd:(5 + c) * hd]
        wkv_s[c, wb + ds:, :] = jnp.zeros((nw - wb - ds, hd), F32)
    s_w = _dot_nt(qs, wkv_s[0].astype(BF16)) + bankw_ref[...]
    w_col = lax.broadcasted_iota(I32, (rows, nw), 1)
    valid_w = ((w_col - wb) <= t_col) & ((w_col - wb) > (t_col - WINDOW)) & (w_col < wb + ds)
    p_w = _softmax_rows(s_w, valid_w)
    o_w = _dot(p_w.astype(BF16), wkv_s[1].astype(BF16))

    gate = jax.nn.sigmoid(g_ref[...])
    for h in range(nh):
        o_ref[:, h * hd:(h + 1) * hd] = (
            gate[:, 3 * h:3 * h + 1] * o_c[h * ds:(h + 1) * ds, :]
            + gate[:, 3 * h + 1:3 * h + 2] * o_s[h]
            + gate[:, 3 * h + 2:3 * h + 3] * o_w[h * ds:(h + 1) * ds, :])


def _nsa_sample(page_table, z, cmp_k, cmp_v, win_buf, cache, bank_c, bank_w, bank_s, bank_t,
                cover, expand, *, row0, ds, chunk=2048):
    n_seq, n_pages = page_table.shape
    page = cache.shape[1] // 2
    hd = NSA_HEAD_DIM
    nh = NSA_HEADS
    rows = nh * ds
    past = n_pages * page
    assert past % chunk == 0 and row0 % ds == 0 and chunk // SLC_BLOCK <= V7X_LANES
    n_slc_blocks = -(-(past + ds) // SLC_BLOCK)
    tail = V7X_LANES
    n_groups = past // chunk + 1
    nc = cmp_k.shape[1]
    wb2 = win_buf.shape[1]
    nw = bank_w.shape[1]
    r0 = row0 // ds
    body = functools.partial(_nsa_sample_body, n_seq=n_seq, n_pages=n_pages, page=page,
                             chunk=chunk, n_slc_blocks=n_slc_blocks)
    const2 = lambda s, pt: (0, 0)
    grid_spec = pltpu.PrefetchScalarGridSpec(
        num_scalar_prefetch=1,
        grid=(n_seq,),
        in_specs=[
            pl.BlockSpec((ds, nh * hd), lambda s, pt: (r0 + s, Z_Q // (nh * hd))),
            pl.BlockSpec((ds, V7X_LANES), lambda s, pt: (r0 + s, Z_G // V7X_LANES)),
            pl.BlockSpec((ds, 6 * hd), lambda s, pt: (r0 + s, Z_KV // (6 * hd))),
            pl.BlockSpec((1, nc, hd), lambda s, pt: (s, 0, 0)),
            pl.BlockSpec((1, nc, hd), lambda s, pt: (s, 0, 0)),
            pl.BlockSpec((1, wb2, hd), lambda s, pt: (s, 0, 0)),
            pl.BlockSpec(memory_space=pl.ANY),
            pl.BlockSpec(bank_c.shape, const2),
            pl.BlockSpec(bank_w.shape, const2),
            pl.BlockSpec(bank_s.shape, const2),
            pl.BlockSpec(bank_t.shape, const2),
            pl.BlockSpec(cover.shape, const2),
            pl.BlockSpec(expand.shape, const2),
        ],
        out_specs=pl.BlockSpec((ds, nh * hd), lambda s, pt: (s, 0)),
        scratch_shapes=[
            pltpu.VMEM((2, 2 * (past + tail), hd), F32),
            pltpu.SemaphoreType.DMA((2,)),
            pltpu.VMEM((rows, hd), BF16),
            pltpu.VMEM((2, nw, hd), F32),
            pltpu.VMEM((n_groups * ds, chunk), F32),
            pltpu.VMEM((nh, ds, 1), F32),
            pltpu.VMEM((nh, ds, 1), F32),
            pltpu.VMEM((nh, ds, hd), F32),
        ],
    )
    return pl.pallas_call(
        body,
        grid_spec=grid_spec,
        out_shape=jax.ShapeDtypeStruct((n_seq * ds, nh * hd), F32),
        compiler_params=_cparams(("arbitrary",)),
        name="nsa_sample",
    )(page_table, z, z, z, cmp_k, cmp_v, win_buf, cache, bank_c, bank_w, bank_s, bank_t,
      cover, expand)


def _conv_silu(g, g1, g2, u, cw_ref, cb_ref, col0, n_valid):
    cw = cw_ref[...]
    conv = cb_ref[...] + cw[0:1] * g2 + cw[1:2] * g1 + cw[2:3] * g
    act = jax.nn.silu(conv) * u
    col = col0 + lax.broadcasted_iota(I32, act.shape, act.ndim - 1)
    return jnp.where(col < n_valid, act, 0.0)


def _ffn_up_prompt_body(a_ref, wg_ref, wu_ref, cw_ref, cb_ref, act_ref, tail_ref,
                        wg_s, wu_s, carry_s, *, tiles_per_seq, n_valid):
    n = pl.program_id(0)
    m = pl.program_id(1)
    tm, tn = act_ref.shape

    @pl.when(m == 0)
    def _cast():
        wg_s[...] = wg_ref[...].astype(BF16)
        wu_s[...] = wu_ref[...].astype(BF16)

    @pl.when(m % tiles_per_seq == 0)
    def _seq_start():
        carry_s[...] = jnp.zeros_like(carry_s)

    a = a_ref[...]
    g = _dot(a, wg_s[...])
    u = _dot(a, wu_s[...])
    prev = carry_s[...]
    row = lax.broadcasted_iota(I32, (tm, tn), 0)
    last = V7X_SUBLANES - 1
    g1 = jnp.where(row == 0, prev[last:last + 1], pltpu.roll(g, 1, 0))
    g2 = jnp.where(row == 0, prev[last - 1:last],
                   jnp.where(row == 1, prev[last:last + 1], pltpu.roll(g, 2, 0)))
    act_ref[...] = _conv_silu(g, g1, g2, u, cw_ref, cb_ref, n * tn, n_valid).astype(act_ref.dtype)
    carry_s[...] = g[tm - V7X_SUBLANES:tm, :]
    tail_ref[0] = g[tm - V7X_SUBLANES:tm, :]


def _ffn_up_prompt(hn, w_gate, w_up, conv_w, conv_b, *, rows, seq, tm=1024, tn=256):
    d, dff = w_gate.shape
    n_blocks = pl.cdiv(dff, tn)
    assert rows % tm == 0 and seq % tm == 0
    body = functools.partial(_ffn_up_prompt_body, tiles_per_seq=seq // tm, n_valid=dff)
    return pl.pallas_call(
        body,
        grid=(n_blocks, rows // tm),
        in_specs=[
            pl.BlockSpec((tm, d), lambda n, m: (m, 0)),
            pl.BlockSpec((d, tn), lambda n, m: (0, n)),
            pl.BlockSpec((d, tn), lambda n, m: (0, n)),
            pl.BlockSpec((conv_w.shape[0], tn), lambda n, m: (0, n)),
            pl.BlockSpec((1, tn), lambda n, m: (0, n)),
        ],
        out_specs=[
            pl.BlockSpec((tm, tn), lambda n, m: (m, n)),
            pl.BlockSpec((1, V7X_SUBLANES, tn), lambda n, m: (m, 0, n)),
        ],
        out_shape=[
            jax.ShapeDtypeStruct((rows, dff), BF16),
            jax.ShapeDtypeStruct((rows // tm, V7X_SUBLANES, dff), F32),
        ],
        scratch_shapes=[pltpu.VMEM((d, tn), BF16), pltpu.VMEM((d, tn), BF16),
                        pltpu.VMEM((V7X_SUBLANES, tn), F32)],
        compiler_params=_cparams(("arbitrary", "arbitrary")),
        name="ffn_up_prompt",
    )(hn, w_gate, w_up, conv_w, conv_b.reshape(1, dff))


def _ffn_up_sample_body(a_ref, wg_ref, wu_ref, cw_ref, cb_ref, st_ref, act_ref, st_out_ref,
                        *, ds, n_valid):
    n = pl.program_id(0)
    tm, tn = act_ref.shape
    nseq = tm // ds
    a = a_ref[...]
    g = _dot(a, wg_ref[...].astype(BF16)).reshape(nseq, ds, tn)
    u = _dot(a, wu_ref[...].astype(BF16)).reshape(nseq, ds, tn)
    st = st_ref[...]
    s0, s1 = st[:, 0:1, :], st[:, 1:2, :]
    t = lax.broadcasted_iota(I32, (nseq, ds, tn), 1)
    g1 = jnp.where(t == 0, s1, pltpu.roll(g, 1, 1))
    g2 = jnp.where(t == 0, s0, jnp.where(t == 1, s1, pltpu.roll(g, 2, 1)))
    cw = cw_ref[...]
    conv = cb_ref[...] + cw[0:1] * g2 + cw[1:2] * g1 + cw[2:3] * g
    act = jax.nn.silu(conv) * u
    col = n * tn + lax.broadcasted_iota(I32, act.shape, 2)
    act = jnp.where(col < n_valid, act, 0.0)
    act_ref[...] = act.reshape(tm, tn).astype(act_ref.dtype)
    st_out_ref[...] = g[:, ds - 2:ds, :]


def _ffn_up_sample(hn, w_gate, w_up, conv_w, conv_b, state, *, row0, rows, ds, tn=256):
    d, dff = w_gate.shape
    n_blocks = pl.cdiv(dff, tn)
    assert row0 % rows == 0 and rows % ds == 0 and ds == V7X_SUBLANES
    nseq = rows // ds
    r0 = row0 // rows
    body = functools.partial(_ffn_up_sample_body, ds=ds, n_valid=dff)
    return pl.pallas_call(
        body,
        grid=(n_blocks,),
        in_specs=[
            pl.BlockSpec((rows, d), lambda n: (r0, 0)),
            pl.BlockSpec((d, tn), lambda n: (0, n)),
            pl.BlockSpec((d, tn), lambda n: (0, n)),
            pl.BlockSpec((conv_w.shape[0], tn), lambda n: (0, n)),
            pl.BlockSpec((1, tn), lambda n: (0, n)),
            pl.BlockSpec((nseq, state.shape[1], tn), lambda n: (0, 0, n)),
        ],
        out_specs=[
            pl.BlockSpec((rows, tn), lambda n: (0, n)),
            pl.BlockSpec((nseq, state.shape[1], tn), lambda n: (0, 0, n)),
        ],
        out_shape=[
            jax.ShapeDtypeStruct((rows, dff), BF16),
            jax.ShapeDtypeStruct(state.shape, F32),
        ],
        compiler_params=_cparams(("parallel",)),
        name="ffn_up_sample",
    )(hn, w_gate, w_up, conv_w, conv_b.reshape(1, dff), state)


def _layout_w_in(w_in):
    d = w_in.shape[0]
    nq = NSA_HEADS * NSA_HEAD_DIM
    nkv = 6 * NSA_HEAD_DIM
    ng = 3 * NSA_HEADS
    o_kv, o_g = nq, nq + nkv
    o_qa = o_g + ng
    o_kva = o_qa + MLA_Q_LORA
    o_m = o_kva + MLA_KV_LORA + MLA_ROPE
    d_model = (w_in.shape[1] - o_m) // 2

    def zeros(n):
        return jnp.zeros((d, n), w_in.dtype)

    parts = [
        w_in[:, 0:nq],
        w_in[:, o_g:o_g + ng], zeros(Z_QA - Z_G - ng),
        w_in[:, o_qa:o_qa + MLA_Q_LORA],
        w_in[:, o_kv:o_kv + nkv],
        w_in[:, o_kva + MLA_KV_LORA:o_kva + MLA_KV_LORA + MLA_ROPE],
        zeros(Z_CKV - Z_KPE - MLA_ROPE),
        w_in[:, o_kva:o_kva + MLA_KV_LORA],
        w_in[:, o_m:o_m + 2 * d_model],
    ]
    out = jnp.concatenate(parts, axis=1).astype(BF16)
    assert out.shape[1] == Z_WIDTH and Z_MB - Z_MA == d_model
    return out


def _rope_tables(pos):
    half = MLA_ROPE // 2
    inv = ROPE_THETA ** (-jnp.arange(half, dtype=F32) / half)
    ang = pos.astype(F32)[:, None] * inv
    cos, sin = jnp.cos(ang), jnp.sin(ang)
    pad = jnp.zeros((pos.shape[0], V7X_LANES - MLA_ROPE), F32)
    return (jnp.concatenate([cos, cos, pad], axis=1), jnp.concatenate([-sin, sin, pad], axis=1))


def kernel(x_prompt, x_sample, cache_mla, cache_nsa_cmp_kv, cache_nsa_slc_kv, cache_nsa_win_kv,
           state_ffn_conv, page_table, rel_bias, norm_attn, w_in, nsa_phi_pos, nsa_phi_w1, nsa_phi_w2,
           mla_q_norm, mla_w_uq, mla_kv_norm, mla_w_ukv, w_br_nsa, w_br_mla, w_out, norm_ffn,
           w_ffn_gate, w_ffn_up, ffn_conv_w, ffn_conv_b, w_ffn_down, norm_final):
    depth = norm_attn.shape[0]
    assert depth == 1, "the layer loop is written for the single-layer configuration"
    lyr = 0
    b, seq, d = x_prompt.shape
    db, ds, _ = x_sample.shape
    mp, msm = b * seq, db * ds
    m = mp + msm
    n_pool, page = cache_mla.shape[1], cache_mla.shape[2]
    n_pages = page_table.shape[1]
    past = n_pages * page
    hd = NSA_HEAD_DIM
    nh = NSA_HEADS
    assert seq % Q_BLOCK == 0 and ds == V7X_SUBLANES and past >= WINDOW
    assert cache_nsa_win_kv.shape[2] == WINDOW and seq >= WINDOW

    x_all = jnp.concatenate([x_prompt.reshape(mp, d), x_sample.reshape(msm, d)], axis=0)
    xn = _rmsnorm(x_all, norm_attn[lyr], BF16, name="norm_attn")
    z = _matmul(xn, _layout_w_in(w_in[lyr]), tm=1024, tn=1280, tk=2048, out_dtype=F32, name="in_proj")

    pos_all = jnp.concatenate([jnp.tile(jnp.arange(seq, dtype=I32), b),
                               jnp.tile(past + jnp.arange(ds, dtype=I32), db)])
    cs, sn = _rope_tables(pos_all)
    w_uq = mla_w_uq[lyr].reshape(MLA_Q_LORA, MLA_HEADS, MLA_NOPE + MLA_ROPE)
    w_uq_p = jnp.concatenate([
        w_uq[:, :, :MLA_NOPE].reshape(MLA_Q_LORA, MLA_HEADS * MLA_NOPE),
        jnp.pad(w_uq[:, :, MLA_NOPE:], ((0, 0), (0, 0), (0, V7X_LANES - MLA_ROPE))).reshape(
            MLA_Q_LORA, MLA_HEADS * V7X_LANES)], axis=1).astype(BF16)
    w_ukv = mla_w_ukv[lyr]
    wk = jnp.transpose(w_ukv[:, :, :MLA_NOPE], (1, 2, 0)).astype(BF16)
    w_uv = jnp.transpose(w_ukv[:, :, MLA_NOPE:], (1, 0, 2)).astype(BF16)
    q_lat, q_pe, lat = _mla_project(z, cs, sn, mla_q_norm[lyr], w_uq_p, mla_kv_norm[lyr], wk)

    pt = page_table + lyr * n_pool
    lat_w = MLA_KV_LORA + MLA_ROPE
    o_lat_p = _mla_prompt(q_lat, q_pe, lat, batch=b, seq=seq)
    cache_mla_t = jnp.swapaxes(cache_mla, 2, 3).reshape(depth * n_pool, lat_w, page)
    o_lat_s = _mla_sample(pt, q_lat, q_pe, lat, cache_mla_t, row0=mp, ds=ds)
    o_mla = jnp.concatenate([_mla_uv(o_lat_p, w_uv, name="mla_uv_prompt"),
                             _mla_uv(o_lat_s, w_uv, name="mla_uv_sample")], axis=0)

    pad = WINDOW + Q_BLOCK
    i_q = np.arange(Q_BLOCK)[:, None]
    bank = _bias_bank(rel_bias, _rel_bucket_np(i_q + pad - np.arange(8 * V7X_LANES)[None, :]), "bias_bank_prompt")
    bank = jnp.transpose(bank.reshape(nh, Q_BLOCK, 8, V7X_LANES), (2, 0, 1, 3)).reshape(
        8, nh * Q_BLOCK, V7X_LANES)
    nc_p = seq // CMP_STRIDE
    qpos_p = np.arange(seq)[:, None]
    bank_cp = _bias_bank(rel_bias, _rel_bucket_np(
        qpos_p - (np.arange(nc_p)[None, :] * CMP_STRIDE + CMP_BLOCK - 1)), "bias_bank_prompt_cmp")
    ns_p = -(-seq // SLC_BLOCK)

    def cover_matrix(n_cmp, n_cmp_pad, n_slc, n_slc_pad):
        c_start = np.arange(n_cmp_pad)[:, None] * CMP_STRIDE
        s_start = np.arange(n_slc_pad)[None, :] * SLC_BLOCK
        cov = (c_start < s_start + SLC_BLOCK) & (c_start + CMP_BLOCK > s_start)
        cov &= (np.arange(n_cmp_pad)[:, None] < n_cmp) & (np.arange(n_slc_pad)[None, :] < n_slc)
        return jnp.asarray(cov.astype(np.float32), dtype=BF16)

    cover_p = cover_matrix(seq // CMP_STRIDE - 1, nc_p, ns_p, V7X_LANES)
    expand_p = jnp.asarray(
        (np.arange(seq)[None, :] // SLC_BLOCK == np.arange(V7X_LANES)[:, None]).astype(np.float32), dtype=BF16)
    phi = nsa_phi_pos[lyr].reshape(CMP_BLOCK, 2 * hd)
    half = CMP_STRIDE * hd
    w1 = jnp.concatenate([nsa_phi_w1[lyr][:, :half], nsa_phi_w1[lyr][:, half:]], axis=2).astype(BF16)
    w2 = nsa_phi_w2[lyr]
    cmp_k, cmp_v = _cmp_prompt(z, phi, w1, w2, batch=b, seq=seq)
    win_new = z[:, Z_KV + 4 * hd:Z_KV + 6 * hd]
    win_pad = jnp.pad(win_new[:mp].reshape(b, seq, 2 * hd), ((0, 0), (pad, 0), (0, 0)))
    o_nsa_p = _nsa_prompt(z, cmp_k, cmp_v, win_pad, bank, bank_cp, cover_p, expand_p, batch=b, seq=seq)

    cache_cmp = cache_nsa_cmp_kv.reshape(depth * n_pool, 2 * page, hd)
    cache_slc = cache_nsa_slc_kv.reshape(depth * n_pool, 2 * page, hd)
    cmp_ks, cmp_vs = _cmp_sample(pt, z, cache_cmp, phi, w1, w2, row0=mp, ds=ds)
    nc_s = cmp_ks.shape[1]
    chunk = 2048
    t_q = np.arange(ds)[:, None]
    wb = cache_nsa_win_kv.shape[2]
    nw = -(-(wb + ds) // V7X_LANES) * V7X_LANES

    def sample_bank(dist, name):
        return _bias_bank(rel_bias, _rel_bucket_np(dist), name).reshape(nh * ds, dist.shape[1])

    bank_cs = sample_bank(past + t_q - (np.arange(nc_s)[None, :] * CMP_STRIDE + CMP_BLOCK - 1), "bias_bank_sample_cmp")
    bank_ws = sample_bank(t_q + wb - np.arange(nw)[None, :], "bias_bank_sample_win")
    bank_ss = sample_bank(past + t_q - (past - chunk + np.arange(chunk)[None, :]), "bias_bank_sample_slc")
    bank_ts = sample_bank(t_q - np.arange(V7X_LANES)[None, :], "bias_bank_sample_new")
    ns_s = -(-(past + ds) // SLC_BLOCK)
    ns_s_pad = -(-ns_s // V7X_LANES) * V7X_LANES
    n_cmp_s = -(-(past + ds) // CMP_STRIDE) - CMP_BLOCK // CMP_STRIDE + 1
    cover_s = cover_matrix(n_cmp_s, nc_s, ns_s, ns_s_pad)
    expand_s = jnp.asarray(
        (np.arange(chunk)[None, :] // SLC_BLOCK == np.arange(V7X_LANES)[:, None]).astype(np.float32), dtype=BF16)
    o_nsa_s = _nsa_sample(pt, z, cmp_ks, cmp_vs, cache_nsa_win_kv[lyr].reshape(db, 2 * wb, hd), cache_slc,
                          bank_cs, bank_ws, bank_ss, bank_ts, cover_s, expand_s,
                          row0=mp, ds=ds, chunk=chunk)
    o_nsa = jnp.concatenate([o_nsa_p, o_nsa_s], axis=0)

    t1 = _matmul(o_nsa, w_br_nsa[lyr], tm=1024, tn=512, tk=2048, out_dtype=F32, extras=[(z, Z_MA)],
                 epilogue=lambda acc, zm: jax.nn.sigmoid(zm) * acc, name="merge_nsa")
    y = _matmul(o_mla, w_br_mla[lyr], tm=1024, tn=512, tk=2048, out_dtype=BF16,
                extras=[(z, Z_MB), (t1, 0)],
                epilogue=lambda acc, zm, t: t + jax.nn.sigmoid(zm) * acc, name="merge_mla")
    h1 = _matmul(y, w_out[lyr], tm=1024, tn=1024, tk=2048, out_dtype=F32, extras=[(x_all, 0)],
                 epilogue=lambda acc, x: x + acc, name="out_proj")

    hn = _rmsnorm(h1, norm_ffn[lyr], BF16, name="norm_ffn")
    dff = w_ffn_gate.shape[2]
    assert ffn_conv_w.shape[1] == 3 and state_ffn_conv.shape[2] == 2
    tm_f = 1024
    act_p, tails = _ffn_up_prompt(hn, w_ffn_gate[lyr], w_ffn_up[lyr], ffn_conv_w[lyr], ffn_conv_b[lyr],
                                  rows=mp, seq=seq, tm=tm_f)
    act_s, conv_s = _ffn_up_sample(hn, w_ffn_gate[lyr], w_ffn_up[lyr], ffn_conv_w[lyr], ffn_conv_b[lyr],
                                   state_ffn_conv[lyr], row0=mp, rows=msm, ds=ds)
    add = lambda acc, h: h + acc
    h2_p = _matmul(act_p, w_ffn_down[lyr], tm=1024, tn=1024, tk=1024, out_dtype=F32, extras=[(h1, 0)],
                   epilogue=add, name="ffn_down_prompt")
    h2_s = _matmul(act_s, w_ffn_down[lyr], tm=1024, tn=1024, tk=1024, out_dtype=F32, extras=[(h1, 0, mp)],
                   epilogue=add, name="ffn_down_sample")
    y_p = _rmsnorm(h2_p, norm_final, F32, name="norm_final_prompt").reshape(b, seq, d)
    y_s = _rmsnorm(h2_s, norm_final, F32, name="norm_final_sample").reshape(db, ds, d)

    kv_tail = (2, 1, hd)
    kv_new = z[:, Z_KV:Z_KV + 6 * hd]
    kv_p = kv_new[:mp].reshape(b, seq, 3, *kv_tail)
    kv_s = kv_new[mp:].reshape(db, ds, 3, *kv_tail)
    win_all = jnp.concatenate([cache_nsa_win_kv[lyr], kv_s[:, :, 2]], axis=1)
    tps = seq // tm_f
    conv_p = tails.reshape(b, tps, V7X_SUBLANES, dff)[:, tps - 1, V7X_SUBLANES - 2:, :]
    return (y_p, y_s,
            lat[:mp].reshape(1, b, seq, lat_w), lat[mp:].reshape(1, db, ds, lat_w),
            kv_p[None, :, :, 0], kv_s[None, :, :, 0],
            kv_p[None, :, :, 1], kv_s[None, :, :, 1],
            kv_p[None, :, seq - min(WINDOW, seq):, 2],
            win_all[None, :, wb + ds - min(WINDOW, wb + ds):],
            conv_p[None], conv_s[None])
```

```python
import functools
import math

import numpy as np
import jax
import jax.numpy as jnp
from jax import lax
from jax.experimental import pallas as pl
from jax.experimental.pallas import tpu as pltpu

F32, BF16, I32 = jnp.float32, jnp.bfloat16, jnp.int32

V7X_LANES = 128
V7X_SUBLANES = 8
V7X_VMEM_BYTES = 64 * 1024 * 1024
VMEM_LIMIT = V7X_VMEM_BYTES - 8 * 1024 * 1024

NSA_HEADS = 16
NSA_HEAD_DIM = 128
CMP_BLOCK = 32
CMP_STRIDE = 16
SLC_BLOCK = 64
N_SELECT = 16
WINDOW = 512
MLA_HEADS = 16
MLA_Q_LORA = 768
MLA_KV_LORA = 512
MLA_NOPE = 128
MLA_ROPE = 64
MLA_V = 128
MLA_SCALE = (MLA_NOPE + MLA_ROPE) ** -0.5
ROPE_THETA = 10000.0
REL_BUCKETS = 32
REL_MAX_DIST = 128
RMS_EPS = 1e-6
Q_BLOCK = 128
NEG_INF = -1e30
FORCED_SCORE = 1e30
PAD_SCORE = -3e38

Z_Q = 0
Z_G = 2048
Z_QA = 2304
Z_KV = 3072
Z_KPE = 3840
Z_CKV = 4096
Z_MA = 4608
Z_MB = 8704
Z_WIDTH = 12800


def _cparams(sem, vmem=VMEM_LIMIT):
    return pltpu.CompilerParams(dimension_semantics=sem, vmem_limit_bytes=vmem)


def _dot(a, b):
    return jnp.dot(a, b, preferred_element_type=F32)


def _dot_nt(a, b):
    return lax.dot_general(a, b, (((1,), (1,)), ((), ())), preferred_element_type=F32)


def _split_bf16(x):
    hi = x.astype(BF16)
    lo = (x - hi.astype(F32)).astype(BF16)
    return hi, lo


def _rmsnorm_body(x_ref, g_ref, o_ref):
    x = x_ref[...]
    ms = jnp.mean(x * x, axis=-1, keepdims=True)
    o_ref[...] = (x * lax.rsqrt(ms + RMS_EPS) * g_ref[...]).astype(o_ref.dtype)


def _rmsnorm(x, g, out_dtype, *, row0=0, rows=None, tm=256, name="rmsnorm"):
    m_all, d = x.shape
    rows = m_all if rows is None else rows
    assert rows % tm == 0 and row0 % tm == 0
    b0 = row0 // tm
    return pl.pallas_call(
        _rmsnorm_body,
        grid=(rows // tm,),
        in_specs=[pl.BlockSpec((tm, d), lambda i: (b0 + i, 0)),
                  pl.BlockSpec((1, d), lambda i: (0, 0))],
        out_specs=pl.BlockSpec((tm, d), lambda i: (i, 0)),
        out_shape=jax.ShapeDtypeStruct((rows, d), out_dtype),
        compiler_params=_cparams(("parallel",)),
        name=name,
    )(x, g.reshape(1, d))


def _mm_body(*refs, nk, tk, k_valid, n_extra, epilogue):
    a_ref, w_ref = refs[0], refs[1]
    extra = refs[2:2 + n_extra]
    o_ref, acc_ref = refs[2 + n_extra], refs[3 + n_extra]
    k = pl.program_id(2)

    @pl.when(k == 0)
    def _init():
        acc_ref[...] = jnp.zeros_like(acc_ref)

    a = a_ref[...].astype(BF16)
    w = w_ref[...].astype(BF16)
    if k_valid % tk:
        row = lax.broadcasted_iota(I32, w.shape, 0) + k * tk
        w = jnp.where(row < k_valid, w, jnp.zeros_like(w))
        col = lax.broadcasted_iota(I32, a.shape, 1) + k * tk
        a = jnp.where(col < k_valid, a, jnp.zeros_like(a))
    acc_ref[...] += _dot(a, w)

    @pl.when(k == nk - 1)
    def _fin():
        o_ref[...] = epilogue(acc_ref[...], *(e[...] for e in extra)).astype(o_ref.dtype)


def _matmul(a, w, *, tm, tn, tk, out_dtype, a_col0=0, extras=(), epilogue=None, name):
    m = a.shape[0]
    k_valid, n = w.shape
    nk = pl.cdiv(k_valid, tk)
    assert m % tm == 0 and n % tn == 0 and a_col0 % tk == 0
    assert a.shape[1] >= a_col0 + k_valid
    if epilogue is None:
        epilogue = lambda acc: acc
    ka0 = a_col0 // tk
    in_specs = [pl.BlockSpec((tm, tk), lambda i, j, k: (i, ka0 + k)),
                pl.BlockSpec((tk, tn), lambda i, j, k: (k, j))]
    args = [a, w]
    for arr, col0, *rest in extras:
        row0 = rest[0] if rest else 0
        assert col0 % tn == 0 and row0 % tm == 0
        in_specs.append(pl.BlockSpec((tm, tn), lambda i, j, k, c=col0 // tn, r=row0 // tm: (r + i, c + j)))
        args.append(arr)
    body = functools.partial(_mm_body, nk=nk, tk=tk, k_valid=k_valid, n_extra=len(extras),
                             epilogue=epilogue)
    return pl.pallas_call(
        body,
        grid=(m // tm, n // tn, nk),
        in_specs=in_specs,
        out_specs=pl.BlockSpec((tm, tn), lambda i, j, k: (i, j)),
        out_shape=jax.ShapeDtypeStruct((m, n), out_dtype),
        scratch_shapes=[pltpu.VMEM((tm, tn), F32)],
        compiler_params=_cparams(("parallel", "parallel", "arbitrary")),
        name=name,
    )(*args)


def _rope_tile(x, cs, sn):
    lane = lax.broadcasted_iota(I32, x.shape, x.ndim - 1) & (V7X_LANES - 1)
    half = MLA_ROPE // 2
    rot = jnp.where(lane < half, pltpu.roll(x, V7X_LANES - half, x.ndim - 1),
                    pltpu.roll(x, half, x.ndim - 1))
    return x * cs + rot * sn


def _mla_proj_body(qa_ref, kpe_ref, ckv_ref, cs_ref, sn_ref, qn_ref, wuq_ref, kvn_ref, wk_ref,
                   qlat_ref, qpe_ref, lat_ref):
    qa = qa_ref[...]
    ms = jnp.mean(qa * qa, axis=-1, keepdims=True)
    qan = (qa * lax.rsqrt(ms + RMS_EPS) * qn_ref[...]).astype(BF16)
    q = _dot(qan, wuq_ref[...])
    nope_w = MLA_HEADS * MLA_NOPE
    for h in range(MLA_HEADS):
        qn_h = q[:, h * MLA_NOPE:(h + 1) * MLA_NOPE].astype(BF16)
        qlat_ref[:, h * MLA_KV_LORA:(h + 1) * MLA_KV_LORA] = _dot(qn_h, wk_ref[h])
    cs = cs_ref[...]
    sn = sn_ref[...]
    for h in range(MLA_HEADS):
        sl = slice(h * V7X_LANES, (h + 1) * V7X_LANES)
        qpe_ref[:, sl] = _rope_tile(q[:, nope_w + h * V7X_LANES:nope_w + (h + 1) * V7X_LANES], cs, sn)
    ckv = ckv_ref[...]
    ms2 = jnp.mean(ckv * ckv, axis=-1, keepdims=True)
    lat_ref[:, 0:MLA_KV_LORA] = ckv * lax.rsqrt(ms2 + RMS_EPS) * kvn_ref[...]
    kpe = _rope_tile(kpe_ref[...], cs, sn)
    lat_ref[:, MLA_KV_LORA:MLA_KV_LORA + MLA_ROPE] = kpe[:, 0:MLA_ROPE]


def _mla_project(z, cs, sn, q_norm, w_uq_p, kv_norm, wk, *, tm=256):
    m = z.shape[0]
    lat_w = MLA_KV_LORA + MLA_ROPE
    return pl.pallas_call(
        _mla_proj_body,
        grid=(m // tm,),
        in_specs=[
            pl.BlockSpec((tm, MLA_Q_LORA), lambda i: (i, Z_QA // MLA_Q_LORA)),
            pl.BlockSpec((tm, V7X_LANES), lambda i: (i, Z_KPE // V7X_LANES)),
            pl.BlockSpec((tm, MLA_KV_LORA), lambda i: (i, Z_CKV // MLA_KV_LORA)),
            pl.BlockSpec((tm, V7X_LANES), lambda i: (i, 0)),
            pl.BlockSpec((tm, V7X_LANES), lambda i: (i, 0)),
            pl.BlockSpec((1, MLA_Q_LORA), lambda i: (0, 0)),
            pl.BlockSpec(w_uq_p.shape, lambda i: (0, 0)),
            pl.BlockSpec((1, MLA_KV_LORA), lambda i: (0, 0)),
            pl.BlockSpec(wk.shape, lambda i: (0, 0, 0)),
        ],
        out_specs=[
            pl.BlockSpec((tm, MLA_HEADS * MLA_KV_LORA), lambda i: (i, 0)),
            pl.BlockSpec((tm, MLA_HEADS * V7X_LANES), lambda i: (i, 0)),
            pl.BlockSpec((tm, lat_w), lambda i: (i, 0)),
        ],
        out_shape=[
            jax.ShapeDtypeStruct((m, MLA_HEADS * MLA_KV_LORA), F32),
            jax.ShapeDtypeStruct((m, MLA_HEADS * V7X_LANES), F32),
            jax.ShapeDtypeStruct((m, lat_w), F32),
        ],
        compiler_params=_cparams(("parallel",)),
        name="mla_project",
    )(z, z, z, cs, sn, q_norm.reshape(1, -1), w_uq_p, kv_norm.reshape(1, -1), wk)


def _stack_heads(ref, width, valid=None):
    valid = width if valid is None else valid
    return jnp.concatenate(
        [ref[:, h * width:h * width + valid].astype(BF16) for h in range(MLA_HEADS)], axis=0)


def _mla_prompt_body(qlat_ref, qpe_ref, lat_ref, o_ref, qlat_s, qpe_s, m_s, l_s, acc_s, *, tk):
    qb = pl.program_id(1)
    nq = qlat_ref.shape[0]
    rows = MLA_HEADS * nq
    qlat_s[...] = _stack_heads(qlat_ref, MLA_KV_LORA)
    qpe_s[...] = _stack_heads(qpe_ref, V7X_LANES, MLA_ROPE)
    m_s[...] = jnp.full(m_s.shape, NEG_INF, F32)
    l_s[...] = jnp.zeros(l_s.shape, F32)
    acc_s[...] = jnp.zeros(acc_s.shape, F32)
    qpos = qb * nq + (lax.broadcasted_iota(I32, (rows, tk), 0) & (nq - 1))
    lane = lax.broadcasted_iota(I32, (rows, tk), 1)

    def step(kt, masked):
        k0 = pl.multiple_of(kt * tk, tk)
        kv = lat_ref[pl.ds(k0, tk), :]
        ckv = kv[:, 0:MLA_KV_LORA].astype(BF16)
        kpe = kv[:, MLA_KV_LORA:MLA_KV_LORA + MLA_ROPE].astype(BF16)
        s = (_dot_nt(qlat_s[...], ckv) + _dot_nt(qpe_s[...], kpe)) * MLA_SCALE
        if masked:
            valid = (k0 + lane) <= qpos
            s = jnp.where(valid, s, NEG_INF)
        m_prev = m_s[...]
        m_new = jnp.maximum(m_prev, jnp.max(s, axis=1, keepdims=True))
        alpha = jnp.exp(m_prev - m_new)
        p = jnp.exp(s - m_new)
        if masked:
            p = jnp.where(valid, p, 0.0)
        l_s[...] = alpha * l_s[...] + jnp.sum(p, axis=1, keepdims=True)
        acc_s[...] = alpha * acc_s[...] + _dot(p.astype(BF16), ckv)
        m_s[...] = m_new

    def full_step(kt, carry):
        step(kt, False)
        return carry

    def diag_step(kt, carry):
        step(kt, True)
        return carry

    n_full = (qb * nq) // tk
    n_tiles = ((qb + 1) * nq + tk - 1) // tk
    lax.fori_loop(0, n_full, full_step, 0)
    lax.fori_loop(n_full, n_tiles, diag_step, 0)
    o = acc_s[...] / l_s[...]
    for h in range(MLA_HEADS):
        o_ref[:, h * MLA_KV_LORA:(h + 1) * MLA_KV_LORA] = o[h * nq:(h + 1) * nq, :]


def _mla_prompt(q_lat, q_pe, lat, *, batch, seq, tk=512):
    nqb = seq // Q_BLOCK
    rows = MLA_HEADS * Q_BLOCK
    lat_w = MLA_KV_LORA + MLA_ROPE
    return pl.pallas_call(
        functools.partial(_mla_prompt_body, tk=tk),
        grid=(batch, nqb),
        in_specs=[
            pl.BlockSpec((Q_BLOCK, MLA_HEADS * MLA_KV_LORA), lambda b, q: (b * nqb + q, 0)),
            pl.BlockSpec((Q_BLOCK, MLA_HEADS * V7X_LANES), lambda b, q: (b * nqb + q, 0)),
            pl.BlockSpec((seq, lat_w), lambda b, q: (b, 0)),
        ],
        out_specs=pl.BlockSpec((Q_BLOCK, MLA_HEADS * MLA_KV_LORA), lambda b, q: (b * nqb + q, 0)),
        out_shape=jax.ShapeDtypeStruct((batch * seq, MLA_HEADS * MLA_KV_LORA), F32),
        scratch_shapes=[
            pltpu.VMEM((rows, MLA_KV_LORA), BF16),
            pltpu.VMEM((rows, MLA_ROPE), BF16),
            pltpu.VMEM((rows, 1), F32),
            pltpu.VMEM((rows, 1), F32),
            pltpu.VMEM((rows, MLA_KV_LORA), F32),
        ],
        compiler_params=_cparams(("parallel", "arbitrary")),
        name="mla_prompt",
    )(q_lat, q_pe, lat)


def _page_gather(pt_ref, cache_ref, dst_of, sem, seq, page0, n_pages, start, unroll=1):
    def body(g, carry):
        for u in range(unroll):
            src = cache_ref.at[pt_ref[seq, page0 + g * unroll + u]] if start else cache_ref.at[0]
            cp = pltpu.make_async_copy(src, dst_of(g, u), sem)
            if start:
                cp.start()
            else:
                cp.wait()
        return carry
    lax.fori_loop(0, n_pages // unroll, body, 0)


def _row_pages(buf, slot, rows_per_page):
    return lambda g, u: buf.at[slot, pl.ds((g + u) * rows_per_page, rows_per_page)]


def _mla_sample_body(pt_ref, qlat_ref, qpe_ref, new_ref, cache_ref, o_ref,
                     buf, sem, qlat_s, qpe_s, new_s, m_s, l_s, acc_s,
                     *, n_seq, n_chunks, ppc, page, sub):
    s_id = pl.program_id(0)
    c_id = pl.program_id(1)
    g = s_id * n_chunks + c_id
    slot = g % 2
    ds = qlat_ref.shape[0]
    rows = MLA_HEADS * ds
    pps = sub // page

    def dst_of(slot_):
        return lambda gi, u: buf.at[slot_, gi, :, u * page:(u + 1) * page]

    @pl.when(g == 0)
    def _first():
        _page_gather(pt_ref, cache_ref, dst_of(0), sem.at[0], 0, 0, ppc, True, unroll=pps)

    @pl.when(g + 1 < n_seq * n_chunks)
    def _prefetch():
        g1 = g + 1
        _page_gather(pt_ref, cache_ref, dst_of(g1 % 2), sem.at[g1 % 2], g1 // n_chunks,
                     (g1 % n_chunks) * ppc, ppc, True, unroll=pps)

    @pl.when(c_id == 0)
    def _init():
        qlat_s[...] = _stack_heads(qlat_ref, MLA_KV_LORA)
        qpe_s[...] = _stack_heads(qpe_ref, V7X_LANES, MLA_ROPE)
        m_s[...] = jnp.full(m_s.shape, NEG_INF, F32)
        l_s[...] = jnp.zeros(l_s.shape, F32)
        acc_s[...] = jnp.zeros(acc_s.shape, F32)

    _page_gather(pt_ref, cache_ref, dst_of(slot), sem.at[slot], s_id, 0, ppc, False, unroll=pps)

    def flash(s, valid, pv_of):
        s = s * MLA_SCALE
        if valid is not None:
            s = jnp.where(valid, s, NEG_INF)
        m_prev = m_s[...]
        m_new = jnp.maximum(m_prev, jnp.max(s, axis=1, keepdims=True))
        alpha = jnp.exp(m_prev - m_new)
        p = jnp.exp(s - m_new)
        if valid is not None:
            p = jnp.where(valid, p, 0.0)
        l_s[...] = alpha * l_s[...] + jnp.sum(p, axis=1, keepdims=True)
        acc_s[...] = alpha * acc_s[...] + pv_of(p.astype(BF16))
        m_s[...] = m_new

    def step(j, carry):
        kt = buf[slot, j]
        ckv_t = kt[0:MLA_KV_LORA, :].astype(BF16)
        kpe_t = kt[MLA_KV_LORA:MLA_KV_LORA + MLA_ROPE, :].astype(BF16)
        s = _dot(qlat_s[...], ckv_t) + _dot(qpe_s[...], kpe_t)
        flash(s, None, lambda p: _dot_nt(p, ckv_t))
        return carry

    lax.fori_loop(0, (ppc * page) // sub, step, 0, unroll=True)

    @pl.when(c_id == n_chunks - 1)
    def _finish():
        new_s[...] = jnp.zeros(new_s.shape, F32)
        new_s[0:ds, :] = new_ref[...]
        nk = new_s.shape[0]
        t_q = lax.broadcasted_iota(I32, (rows, nk), 0) & (ds - 1)
        j_k = lax.broadcasted_iota(I32, (rows, nk), 1)
        kv = new_s[...]
        ckv = kv[:, 0:MLA_KV_LORA].astype(BF16)
        kpe = kv[:, MLA_KV_LORA:MLA_KV_LORA + MLA_ROPE].astype(BF16)
        s = _dot_nt(qlat_s[...], ckv) + _dot_nt(qpe_s[...], kpe)
        flash(s, j_k <= t_q, lambda p: _dot(p, ckv))
        o = acc_s[...] / l_s[...]
        for h in range(MLA_HEADS):
            o_ref[:, h * MLA_KV_LORA:(h + 1) * MLA_KV_LORA] = o[h * ds:(h + 1) * ds, :]


def _mla_sample(page_table, q_lat, q_pe, lat, cache_t, *, row0, ds, ppc=32, sub=2048):
    n_seq, n_pages = page_table.shape
    lat_w, page = cache_t.shape[1], cache_t.shape[2]
    assert n_pages % ppc == 0 and (ppc * page) % sub == 0 and sub % page == 0 and row0 % ds == 0
    n_chunks = n_pages // ppc
    rows = MLA_HEADS * ds
    r0 = row0 // ds
    body = functools.partial(_mla_sample_body, n_seq=n_seq, n_chunks=n_chunks, ppc=ppc, page=page,
                             sub=sub)
    grid_spec = pltpu.PrefetchScalarGridSpec(
        num_scalar_prefetch=1,
        grid=(n_seq, n_chunks),
        in_specs=[
            pl.BlockSpec((ds, MLA_HEADS * MLA_KV_LORA), lambda s, c, pt: (r0 + s, 0)),
            pl.BlockSpec((ds, MLA_HEADS * V7X_LANES), lambda s, c, pt: (r0 + s, 0)),
            pl.BlockSpec((ds, lat_w), lambda s, c, pt: (r0 + s, 0)),
            pl.BlockSpec(memory_space=pl.ANY),
        ],
        out_specs=pl.BlockSpec((ds, MLA_HEADS * MLA_KV_LORA), lambda s, c, pt: (s, 0)),
        scratch_shapes=[
            pltpu.VMEM((2, (ppc * page) // sub, lat_w, sub), F32),
            pltpu.SemaphoreType.DMA((2,)),
            pltpu.VMEM((rows, MLA_KV_LORA), BF16),
            pltpu.VMEM((rows, MLA_ROPE), BF16),
            pltpu.VMEM((V7X_LANES, lat_w), F32),
            pltpu.VMEM((rows, 1), F32),
            pltpu.VMEM((rows, 1), F32),
            pltpu.VMEM((rows, MLA_KV_LORA), F32),
        ],
    )
    return pl.pallas_call(
        body,
        grid_spec=grid_spec,
        out_shape=jax.ShapeDtypeStruct((n_seq * ds, MLA_HEADS * MLA_KV_LORA), F32),
        compiler_params=_cparams(("arbitrary", "arbitrary")),
        name="mla_sample",
    )(page_table, q_lat, q_pe, lat, cache_t)


def _mla_uv_body(o_ref, w_ref, out_ref):
    for h in range(MLA_HEADS):
        o_h = o_ref[:, h * MLA_KV_LORA:(h + 1) * MLA_KV_LORA].astype(BF16)
        out_ref[:, h * MLA_V:(h + 1) * MLA_V] = _dot(o_h, w_ref[h]).astype(out_ref.dtype)


def _mla_uv(o_lat, w_uv, *, tm=256, name="mla_uv"):
    m = o_lat.shape[0]
    assert m % tm == 0
    return pl.pallas_call(
        _mla_uv_body,
        grid=(m // tm,),
        in_specs=[pl.BlockSpec((tm, MLA_HEADS * MLA_KV_LORA), lambda i: (i, 0)),
                  pl.BlockSpec(w_uv.shape, lambda i: (0, 0, 0))],
        out_specs=pl.BlockSpec((tm, MLA_HEADS * MLA_V), lambda i: (i, 0)),
        out_shape=jax.ShapeDtypeStruct((m, MLA_HEADS * MLA_V), BF16),
        compiler_params=_cparams(("parallel",)),
        name=name,
    )(o_lat, w_uv)


def _rel_bucket_np(dist):
    n = np.maximum(dist, 0)
    max_exact = REL_BUCKETS // 2
    nf = np.maximum(n, max_exact).astype(np.float32)
    large = max_exact + (np.log(nf / max_exact) / math.log(REL_MAX_DIST / max_exact)
                         * (REL_BUCKETS - max_exact)).astype(np.int32)
    large = np.minimum(large, REL_BUCKETS - 1)
    return np.where(n < max_exact, n, large).astype(np.int32)


def _bias_bank_body(rb_ref, bucket_ref, o_ref):
    h = pl.program_id(0)
    bucket = bucket_ref[...]
    acc = jnp.zeros(bucket.shape, F32)
    for b in range(REL_BUCKETS):
        acc = jnp.where(bucket == b, rb_ref[b, h], acc)
    o_ref[0] = acc


def _bias_bank(rel_bias, bucket, name):
    r, c = bucket.shape
    return pl.pallas_call(
        _bias_bank_body,
        grid=(NSA_HEADS,),
        in_specs=[pl.BlockSpec(memory_space=pltpu.SMEM),
                  pl.BlockSpec((r, c), lambda h: (0, 0))],
        out_specs=pl.BlockSpec((1, r, c), lambda h: (h, 0, 0)),
        out_shape=jax.ShapeDtypeStruct((NSA_HEADS, r, c), F32),
        compiler_params=_cparams(("parallel",)),
        name=name,
    )(rel_bias, jnp.asarray(bucket))


def _gelu_tanh(x):
    return 0.5 * x * (1.0 + jnp.tanh(math.sqrt(2.0 / math.pi) * (x + 0.044715 * x * x * x)))


def _compress_rows(load_rows, n_blocks, phi_ref, w1_ref, w2_ref, flat_s, hi_s, outs):
    hd = NSA_HEAD_DIM
    gc = flat_s.shape[1]
    gb = gc - V7X_SUBLANES
    nh1 = w1_ref.shape[2] // 2
    turn = 0
    for c in range(2):
        phi_c = phi_ref[:, c * hd:(c + 1) * hd]
        phi_flat = jnp.concatenate(
            [jnp.concatenate([phi_c[half * CMP_STRIDE + i:half * CMP_STRIDE + i + 1, :]
                              for i in range(CMP_STRIDE)], axis=1) for half in range(2)]
            + [jnp.zeros((V7X_SUBLANES - 2, CMP_STRIDE * hd), F32)], axis=0)
        pw = _dot(phi_flat.astype(BF16), w1_ref[c])
        const = pw[0:1, 0:nh1] + pw[1:2, nh1:2 * nh1]
        for g0 in range(0, n_blocks, gb):
            slot, turn = turn % 2, turn + 1
            for i in range(CMP_STRIDE):
                x = load_rows(c, g0 * CMP_STRIDE + i, gc, CMP_STRIDE)
                flat_s[slot, :, i * hd:(i + 1) * hd] = x.astype(BF16)
            ab = _dot(flat_s[slot], w1_ref[c])
            hi_s[slot] = ab[:, nh1:2 * nh1]
            hid = _gelu_tanh(ab[0:gb, 0:nh1] + hi_s[slot, 1:gb + 1, :] + const)
            outs[c][0, g0:g0 + gb, :] = _dot(hid.astype(BF16), w2_ref[c].astype(BF16))


def _cmp_prompt_body(x_ref, phi_ref, w1_ref, w2_ref, ck_ref, cv_ref, xpad_s, flat_s, hi_s, *, n_blocks):
    seq = x_ref.shape[0]
    hd = NSA_HEAD_DIM
    for c in range(2):
        xpad_s[c, 0:seq, :] = x_ref[:, c * hd:(c + 1) * hd]
        xpad_s[c, seq:, :] = jnp.zeros((xpad_s.shape[1] - seq, hd), F32)
    load = lambda c, first, count, stride: xpad_s[c, pl.ds(first, count, stride=stride), :]
    _compress_rows(load, n_blocks, phi_ref, w1_ref, w2_ref, flat_s, hi_s, (ck_ref, cv_ref))


def _cmp_prompt(z, phi, w1, w2, *, batch, seq):
    hd = NSA_HEAD_DIM
    n_blocks = seq // CMP_STRIDE
    n_chunks = n_blocks + V7X_SUBLANES
    pad_rows = n_chunks * CMP_STRIDE
    out = jax.ShapeDtypeStruct((batch, n_blocks, hd), F32)
    return pl.pallas_call(
        functools.partial(_cmp_prompt_body, n_blocks=n_blocks),
        grid=(batch,),
        in_specs=[
            pl.BlockSpec((seq, 2 * hd), lambda b: (b, Z_KV // (2 * hd))),
            pl.BlockSpec(phi.shape, lambda b: (0, 0)),
            pl.BlockSpec(w1.shape, lambda b: (0, 0, 0)),
            pl.BlockSpec(w2.shape, lambda b: (0, 0, 0)),
        ],
        out_specs=[pl.BlockSpec((1, n_blocks, hd), lambda b: (b, 0, 0))] * 2,
        out_shape=[out, out],
        scratch_shapes=[pltpu.VMEM((2, pad_rows, hd), F32),
                        pltpu.VMEM((2, n_chunks, CMP_STRIDE * hd), BF16),
                        pltpu.VMEM((2, n_chunks, w1.shape[2] // 2), F32)],
        compiler_params=_cparams(("parallel",)),
        name="nsa_cmp_prompt",
    )(z, phi, w1, w2)


def _cmp_sample_body(pt_ref, new_ref, cache_ref, phi_ref, w1_ref, w2_ref, ck_ref, cv_ref,
                     buf, sem, flat_s, hi_s, *, n_seq, n_pages, page, n_blocks):
    s_id = pl.program_id(0)
    slot = s_id % 2
    past = n_pages * page
    ds = new_ref.shape[0]

    hd = NSA_HEAD_DIM

    @pl.when(s_id == 0)
    def _first():
        _page_gather(pt_ref, cache_ref, _row_pages(buf, 0, 2 * page), sem.at[0], 0, 0, n_pages, True)

    @pl.when(s_id + 1 < n_seq)
    def _prefetch():
        nxt = (s_id + 1) % 2
        _page_gather(pt_ref, cache_ref, _row_pages(buf, nxt, 2 * page), sem.at[nxt], s_id + 1, 0,
                     n_pages, True)

    _store_kv_rows(buf, slot, past, new_ref[:, 0:2 * hd])
    buf[slot, 2 * (past + ds):, :] = jnp.zeros((buf.shape[1] - 2 * (past + ds), hd), F32)
    _page_gather(pt_ref, cache_ref, _row_pages(buf, slot, 2 * page), sem.at[slot], s_id, 0, n_pages, False)
    load = lambda c, first, count, stride: buf[slot, pl.ds(2 * first + c, count, stride=2 * stride), :]
    _compress_rows(load, n_blocks, phi_ref, w1_ref, w2_ref, flat_s, hi_s, (ck_ref, cv_ref))


def _store_kv_rows(buf, slot, first_token, kv):
    hd = NSA_HEAD_DIM
    for c in range(2):
        buf[slot, pl.ds(2 * first_token + c, kv.shape[0], stride=2), :] = kv[:, c * hd:(c + 1) * hd]


def _cmp_sample(page_table, z, cache, phi, w1, w2, *, row0, ds):
    n_seq, n_pages = page_table.shape
    page = cache.shape[1] // 2
    hd = NSA_HEAD_DIM
    past = n_pages * page
    n_chunk = -(-(past + ds) // CMP_STRIDE)
    n_blocks = n_chunk - CMP_BLOCK // CMP_STRIDE + 1
    gb = min(n_blocks, 512)
    assert n_blocks % gb == 0 and gb % V7X_SUBLANES == 0 and row0 % ds == 0
    gc = gb + V7X_SUBLANES
    buf_rows = (n_blocks + V7X_SUBLANES) * CMP_STRIDE
    r0 = row0 // ds
    out = jax.ShapeDtypeStruct((n_seq, n_blocks, hd), F32)
    grid_spec = pltpu.PrefetchScalarGridSpec(
        num_scalar_prefetch=1,
        grid=(n_seq,),
        in_specs=[
            pl.BlockSpec((ds, 2 * hd), lambda s, pt: (r0 + s, Z_KV // (2 * hd))),
            pl.BlockSpec(memory_space=pl.ANY),
            pl.BlockSpec(phi.shape, lambda s, pt: (0, 0)),
            pl.BlockSpec(w1.shape, lambda s, pt: (0, 0, 0)),
            pl.BlockSpec(w2.shape, lambda s, pt: (0, 0, 0)),
        ],
        out_specs=[pl.BlockSpec((1, n_blocks, hd), lambda s, pt: (s, 0, 0))] * 2,
        scratch_shapes=[
            pltpu.VMEM((2, 2 * buf_rows, hd), F32),
            pltpu.SemaphoreType.DMA((2,)),
            pltpu.VMEM((2, gc, CMP_STRIDE * hd), BF16),
            pltpu.VMEM((2, gc, w1.shape[2] // 2), F32),
        ],
    )
    body = functools.partial(_cmp_sample_body, n_seq=n_seq, n_pages=n_pages, page=page,
                             n_blocks=n_blocks)
    return pl.pallas_call(
        body,
        grid_spec=grid_spec,
        out_shape=[out, out],
        compiler_params=_cparams(("arbitrary",)),
        name="nsa_cmp_sample",
    )(page_table, z, cache, phi, w1, w2)


def _softmax_rows(s, valid):
    s = jnp.where(valid, s, NEG_INF)
    m = jnp.max(s, axis=-1, keepdims=True)
    p = jnp.where(valid, jnp.exp(s - m), 0.0)
    l = jnp.sum(p, axis=-1, keepdims=True)
    return p / jnp.where(l > 0.0, l, 1.0)


def _flash_update(s, valid, v, m_s, l_s, acc_s):
    nh, nq, tk = s.shape
    s = jnp.where(valid, s, NEG_INF)
    m_prev = m_s[...]
    m_new = jnp.maximum(m_prev, jnp.max(s, axis=-1, keepdims=True))
    alpha = jnp.exp(m_prev - m_new)
    p = jnp.where(valid, jnp.exp(s - m_new), 0.0)
    l_s[...] = alpha * l_s[...] + jnp.sum(p, axis=-1, keepdims=True)
    pv = _dot(p.astype(BF16).reshape(nh * nq, tk), v)
    acc_s[...] = alpha * acc_s[...] + pv.reshape(nh, nq, v.shape[1])
    m_s[...] = m_new


def _flash_reset(m_s, l_s, acc_s):
    m_s[...] = jnp.full(m_s.shape, NEG_INF, F32)
    l_s[...] = jnp.zeros(l_s.shape, F32)
    acc_s[...] = jnp.zeros(acc_s.shape, F32)


def _nsa_prompt_body(q_ref, g_ref, ck_ref, cv_ref, slc_ref, win_ref, bank_ref, bankc_ref,
                     cover_ref, expand_ref, o_ref,
                     qs_s, sel_s, oc_s, m_s, l_s, acc_s, os_s, *, tk, n_slc_blocks):
    qb = pl.program_id(1)
    nq = q_ref.shape[0]
    hd = NSA_HEAD_DIM
    nh = NSA_HEADS
    rows = nh * nq
    start = qb * nq
    scale = hd ** -0.5
    qs_s[...] = jnp.concatenate(
        [(q_ref[:, h * hd:(h + 1) * hd] * scale).astype(BF16) for h in range(nh)], axis=0)

    nc = ck_ref.shape[1]
    s_c = _dot_nt(qs_s[...], ck_ref[0].astype(BF16)) + bankc_ref[...].reshape(rows, nc)
    i_row = lax.broadcasted_iota(I32, (rows, nc), 0) & (nq - 1)
    n_col = lax.broadcasted_iota(I32, (rows, nc), 1)
    valid_c = (n_col * CMP_STRIDE + (CMP_BLOCK - 1)) <= (start + i_row)
    p_c = _softmax_rows(s_c, valid_c)
    oc_s[...] = _dot(p_c.astype(BF16), cv_ref[0].astype(BF16))

    ph = p_c[0:nq, :]
    for h in range(1, nh):
        ph = ph + p_c[h * nq:(h + 1) * nq, :]
    hi, lo = _split_bf16(ph)
    imp = _dot(hi, cover_ref[...]) + _dot(lo, cover_ref[...])
    ns_pad = imp.shape[1]
    blk = lax.broadcasted_iota(I32, (nq, ns_pad), 1)
    qpos = start + lax.broadcasted_iota(I32, (nq, ns_pad), 0)
    cur = lax.shift_right_logical(qpos, int(math.log2(SLC_BLOCK)))
    forced = (blk == 0) | (blk == cur) | (blk == cur - 1)
    visible = (blk * SLC_BLOCK) <= qpos
    score = jnp.where(forced, FORCED_SCORE, jnp.where(visible, imp, -1.0))
    score = jnp.where(blk < n_slc_blocks, score, PAD_SCORE)
    rank = jnp.zeros((nq, ns_pad), F32)
    for sp in range(n_slc_blocks):
        col = score[:, sp:sp + 1]
        ahead = (col > score) | ((col == score) & (blk > sp))
        rank = rank + jnp.where(ahead, 1.0, 0.0)
    sel = jnp.where((rank < float(min(N_SELECT, n_slc_blocks))) & (blk < n_slc_blocks), 1.0, 0.0)
    selx = _dot(sel.astype(BF16), expand_ref[...])
    n_kt = selx.shape[1] // tk
    for kt in range(n_kt):
        sel_s[kt] = selx[:, kt * tk:(kt + 1) * tk]

    i3 = lax.broadcasted_iota(I32, (nq, tk), 0)
    j3 = lax.broadcasted_iota(I32, (nq, tk), 1)
    n_bank = bank_ref.shape[0]
    half = tk // V7X_LANES

    def bias_tile(r0_tiles):
        parts = [bank_ref[jnp.clip(r0_tiles + u, 0, n_bank - 1)] for u in range(half)]
        return jnp.concatenate(parts, axis=1)

    _flash_reset(m_s, l_s, acc_s)
    assert nq == V7X_LANES
    org_tiles = qb - (WINDOW + Q_BLOCK) // V7X_LANES

    def slc_step(kt, carry):
        k0 = pl.multiple_of(kt * tk, tk)
        kv = slc_ref[pl.ds(k0, tk), :]
        s = _dot_nt(qs_s[...], kv[:, 0:hd].astype(BF16))
        s = s + bias_tile(kt * half - org_tiles)
        valid = (sel_s[kt] > 0.5) & ((k0 + j3) <= (start + i3))
        s3 = s.reshape(nh, nq, tk)
        _flash_update(s3, valid[None], kv[:, hd:2 * hd].astype(BF16), m_s, l_s, acc_s)
        return carry

    lax.fori_loop(0, (start + nq + tk - 1) // tk, slc_step, 0)
    os_s[...] = acc_s[...] / l_s[...]

    _flash_reset(m_s, l_s, acc_s)
    pad = WINDOW + Q_BLOCK
    for d in range(-(-(pad + nq) // tk)):
        r0 = d * tk
        kv = win_ref[0, pl.ds(start + r0, tk), :]
        kpos = start - pad + r0 + j3
        s = _dot_nt(qs_s[...], kv[:, 0:hd].astype(BF16)) + bias_tile(d * half)
        qp = start + i3
        valid = (kpos <= qp) & (kpos > qp - WINDOW) & (kpos >= 0)
        _flash_update(s.reshape(nh, nq, tk), valid[None], kv[:, hd:2 * hd].astype(BF16),
                      m_s, l_s, acc_s)
    o_w = acc_s[...] / l_s[...]

    gate = jax.nn.sigmoid(g_ref[...])
    for h in range(nh):
        o_ref[:, h * hd:(h + 1) * hd] = (
            gate[:, 3 * h:3 * h + 1] * oc_s[h * nq:(h + 1) * nq, :]
            + gate[:, 3 * h + 1:3 * h + 2] * os_s[h]
            + gate[:, 3 * h + 2:3 * h + 3] * o_w[h])


def _nsa_prompt(z, cmp_k, cmp_v, win_pad, bank, bank_c, cover, expand, *, batch, seq, tk=512):
    nqb = seq // Q_BLOCK
    hd = NSA_HEAD_DIM
    nh = NSA_HEADS
    rows = nh * Q_BLOCK
    nc = cmp_k.shape[1]
    n_slc_blocks = -(-seq // SLC_BLOCK)
    win_span = -(-(WINDOW + 2 * Q_BLOCK) // tk) * tk
    assert seq % tk == 0 and win_pad.shape[1] >= seq - Q_BLOCK + win_span
    body = functools.partial(_nsa_prompt_body, tk=tk, n_slc_blocks=n_slc_blocks)
    return pl.pallas_call(
        body,
        grid=(batch, nqb),
        in_specs=[
            pl.BlockSpec((Q_BLOCK, nh * hd), lambda b, q: (b * nqb + q, Z_Q // (nh * hd))),
            pl.BlockSpec((Q_BLOCK, V7X_LANES), lambda b, q: (b * nqb + q, Z_G // V7X_LANES)),
            pl.BlockSpec((1, nc, hd), lambda b, q: (b, 0, 0)),
            pl.BlockSpec((1, nc, hd), lambda b, q: (b, 0, 0)),
            pl.BlockSpec((seq, 2 * hd), lambda b, q: (b, Z_KV // (2 * hd) + 1)),
            pl.BlockSpec((1,) + win_pad.shape[1:], lambda b, q: (b, 0, 0)),
            pl.BlockSpec(bank.shape, lambda b, q: (0, 0, 0)),
            pl.BlockSpec((nh, Q_BLOCK, nc), lambda b, q: (0, q, 0)),
            pl.BlockSpec(cover.shape, lambda b, q: (0, 0)),
            pl.BlockSpec(expand.shape, lambda b, q: (0, 0)),
        ],
        out_specs=pl.BlockSpec((Q_BLOCK, nh * hd), lambda b, q: (b * nqb + q, 0)),
        out_shape=jax.ShapeDtypeStruct((batch * seq, nh * hd), F32),
        scratch_shapes=[
            pltpu.VMEM((rows, hd), BF16),
            pltpu.VMEM((seq // tk, Q_BLOCK, tk), F32),
            pltpu.VMEM((rows, hd), F32),
            pltpu.VMEM((nh, Q_BLOCK, 1), F32),
            pltpu.VMEM((nh, Q_BLOCK, 1), F32),
            pltpu.VMEM((nh, Q_BLOCK, hd), F32),
            pltpu.VMEM((nh, Q_BLOCK, hd), F32),
        ],
        compiler_params=_cparams(("parallel", "arbitrary")),
        name="nsa_prompt",
    )(z, z, cmp_k, cmp_v, z, win_pad, bank, bank_c, cover, expand)


def _nsa_sample_body(pt_ref, q_ref, g_ref, new_ref, ck_ref, cv_ref, win_ref, cache_ref,
                     bankc_ref, bankw_ref, banks_ref, bankt_ref, cover_ref, expand_ref,
                     o_ref,
                     buf, sem, qs_s, wkv_s, selx_s, m_s, l_s, acc_s,
                     *, n_seq, n_pages, page, chunk, n_slc_blocks):
    s_id = pl.program_id(0)
    slot = s_id % 2
    ds = q_ref.shape[0]
    hd = NSA_HEAD_DIM
    nh = NSA_HEADS
    rows = nh * ds
    past = n_pages * page
    n_chunks = past // chunk
    scale = hd ** -0.5

    @pl.when(s_id == 0)
    def _first():
        _page_gather(pt_ref, cache_ref, _row_pages(buf, 0, 2 * page), sem.at[0], 0, 0, n_pages, True)

    @pl.when(s_id + 1 < n_seq)
    def _prefetch():
        nxt = (s_id + 1) % 2
        _page_gather(pt_ref, cache_ref, _row_pages(buf, nxt, 2 * page), sem.at[nxt], s_id + 1, 0,
                     n_pages, True)

    new_kv = new_ref[...]
    tail = buf.shape[1] // 2 - past
    _store_kv_rows(buf, slot, past, new_kv[:, 2 * hd:4 * hd])
    buf[slot, 2 * (past + ds):, :] = jnp.zeros((2 * (tail - ds), hd), F32)

    def load_kv(first, count):
        return (buf[slot, pl.ds(2 * first, count, stride=2), :].astype(BF16),
                buf[slot, pl.ds(2 * first + 1, count, stride=2), :].astype(BF16))

    qs = jnp.concatenate(
        [(q_ref[:, h * hd:(h + 1) * hd] * scale).astype(BF16) for h in range(nh)], axis=0)
    qs_s[...] = qs
    t_col = lax.broadcasted_iota(I32, (rows, 1), 0) & (ds - 1)

    nc = ck_ref.shape[1]
    s_c = _dot_nt(qs, ck_ref[0].astype(BF16)) + bankc_ref[...]
    n_col = lax.broadcasted_iota(I32, (rows, nc), 1)
    valid_c = (n_col * CMP_STRIDE + (CMP_BLOCK - 1)) <= (past + t_col)
    p_c = _softmax_rows(s_c, valid_c)
    o_c = _dot(p_c.astype(BF16), cv_ref[0].astype(BF16))

    ph = p_c[0:ds, :]
    for h in range(1, nh):
        ph = ph + p_c[h * ds:(h + 1) * ds, :]
    hi, lo = _split_bf16(ph)
    imp = _dot(hi, cover_ref[...]) + _dot(lo, cover_ref[...])
    ns_pad = imp.shape[1]
    blk = lax.broadcasted_iota(I32, (ds, ns_pad), 1)
    qpos = past + lax.broadcasted_iota(I32, (ds, ns_pad), 0)
    cur = lax.shift_right_logical(qpos, int(math.log2(SLC_BLOCK)))
    forced = (blk == 0) | (blk == cur) | (blk == cur - 1)
    visible = (blk * SLC_BLOCK) <= qpos
    score = jnp.where(forced, FORCED_SCORE, jnp.where(visible, imp, -1.0))
    score = jnp.where(blk < n_slc_blocks, score, PAD_SCORE)
    score_t = jnp.concatenate([score, jnp.zeros((V7X_LANES - ds, ns_pad), F32)], axis=0).T
    sp_idx = lax.broadcasted_iota(I32, (ns_pad, ns_pad), 0)
    s_idx = lax.broadcasted_iota(I32, (ns_pad, ns_pad), 1)
    ranks = []
    for t in range(ds):
        c = score_t[:, t:t + 1]
        r = score[t:t + 1, :]
        ahead = (c > r) | ((c == r) & (sp_idx < s_idx))
        ranks.append(jnp.sum(jnp.where(ahead, 1.0, 0.0), axis=0, keepdims=True))
    rank = jnp.concatenate(ranks, axis=0)
    sel = jnp.where((rank < float(min(N_SELECT, n_slc_blocks))) & (blk < n_slc_blocks), 1.0, 0.0)

    bpc = chunk // SLC_BLOCK
    n_groups = selx_s.shape[0] // ds
    lane = lax.broadcasted_iota(I32, (ds, V7X_LANES), 1)
    pieces = []
    for gi in range(n_groups):
        a, off = (gi * bpc) // V7X_LANES, (gi * bpc) % V7X_LANES
        tile = sel[:, a * V7X_LANES:(a + 1) * V7X_LANES]
        if off:
            tile = pltpu.roll(tile, V7X_LANES - off, 1)
        pieces.append(jnp.where(lane < bpc, tile, 0.0))
    selx_s[...] = _dot(jnp.concatenate(pieces, axis=0).astype(BF16), expand_ref[...])

    _page_gather(pt_ref, cache_ref, _row_pages(buf, slot, 2 * page), sem.at[slot], s_id, 0, n_pages, False)
    _flash_reset(m_s, l_s, acc_s)
    far_bias = banks_ref[:, 0:1].reshape(nh, ds, 1)

    def slc_chunk(c, bias):
        k, v = load_kv(pl.multiple_of(c * chunk, chunk), chunk)
        s = _dot_nt(qs_s[...], k).reshape(nh, ds, chunk) + bias
        valid = selx_s[pl.ds(pl.multiple_of(c * ds, ds), ds), :] > 0.5
        _flash_update(s, valid[None], v, m_s, l_s, acc_s)

    def slc_step(c, carry):
        slc_chunk(c, far_bias)
        return carry

    lax.fori_loop(0, n_chunks - 1, slc_step, 0)
    slc_chunk(n_chunks - 1, banks_ref[...].reshape(nh, ds, chunk))

    k_t, v_t = load_kv(past, tail)
    s_t = _dot_nt(qs, k_t) + bankt_ref[...]
    j_t = lax.broadcasted_iota(I32, (ds, tail), 1)
    t_t = lax.broadcasted_iota(I32, (ds, tail), 0)
    valid_t = (selx_s[n_chunks * ds:(n_chunks + 1) * ds, 0:tail] > 0.5) & (j_t <= t_t)
    _flash_update(s_t.reshape(nh, ds, tail), valid_t[None], v_t, m_s, l_s, acc_s)
    o_s = acc_s[...] / l_s[...]

    wb = win_ref.shape[1] // 2
    nw = wkv_s.shape[1]
    for c in range(2):
        wkv_s[c, 0:wb, :] = win_ref[0, pl.ds(c, wb, stride=2), :]
        wkv_s[c, wb:wb + ds, :] = new_kv[:, (4 + c) * hd:(5 + c) * hd]
        wkv_s[c, wb + ds:, :] = jnp.zeros((nw - wb - ds, hd), F32)
    s_w = _dot_nt(qs, wkv_s[0].astype(BF16)) + bankw_ref[...]
    w_col = lax.broadcasted_iota(I32, (rows, nw), 1)
    valid_w = ((w_col - wb) <= t_col) & ((w_col - wb) > (t_col - WINDOW)) & (w_col < wb + ds)
    p_w = _softmax_rows(s_w, valid_w)
    o_w = _dot(p_w.astype(BF16), wkv_s[1].astype(BF16))

    gate = jax.nn.sigmoid(g_ref[...])
    for h in range(nh):
        o_ref[:, h * hd:(h + 1) * hd] = (
            gate[:, 3 * h:3 * h + 1] * o_c[h * ds:(h + 1) * ds, :]
            + gate[:, 3 * h + 1:3 * h + 2] * o_s[h]
            + gate[:, 3 * h + 2:3 * h + 3] * o_w[h * ds:(h + 1) * ds, :])


def _nsa_sample(page_table, z, cmp_k, cmp_v, win_buf, cache, bank_c, bank_w, bank_s, bank_t,
                cover, expand, *, row0, ds, chunk=2048):
    n_seq, n_pages = page_table.shape
    page = cache.shape[1] // 2
    hd = NSA_HEAD_DIM
    nh = NSA_HEADS
    rows = nh * ds
    past = n_pages * page
    assert past % chunk == 0 and row0 % ds == 0 and chunk // SLC_BLOCK <= V7X_LANES
    n_slc_blocks = -(-(past + ds) // SLC_BLOCK)
    tail = V7X_LANES
    n_groups = past // chunk + 1
    nc = cmp_k.shape[1]
    wb2 = win_buf.shape[1]
    nw = bank_w.shape[1]
    r0 = row0 // ds
    body = functools.partial(_nsa_sample_body, n_seq=n_seq, n_pages=n_pages, page=page,
                             chunk=chunk, n_slc_blocks=n_slc_blocks)
    const2 = lambda s, pt: (0, 0)
    grid_spec = pltpu.PrefetchScalarGridSpec(
        num_scalar_prefetch=1,
        grid=(n_seq,),
        in_specs=[
            pl.BlockSpec((ds, nh * hd), lambda s, pt: (r0 + s, Z_Q // (nh * hd))),
            pl.BlockSpec((ds, V7X_LANES), lambda s, pt: (r0 + s, Z_G // V7X_LANES)),
            pl.BlockSpec((ds, 6 * hd), lambda s, pt: (r0 + s, Z_KV // (6 * hd))),
            pl.BlockSpec((1, nc, hd), lambda s, pt: (s, 0, 0)),
            pl.BlockSpec((1, nc, hd), lambda s, pt: (s, 0, 0)),
            pl.BlockSpec((1, wb2, hd), lambda s, pt: (s, 0, 0)),
            pl.BlockSpec(memory_space=pl.ANY),
            pl.BlockSpec(bank_c.shape, const2),
            pl.BlockSpec(bank_w.shape, const2),
            pl.BlockSpec(bank_s.shape, const2),
            pl.BlockSpec(bank_t.shape, const2),
            pl.BlockSpec(cover.shape, const2),
            pl.BlockSpec(expand.shape, const2),
        ],
        out_specs=pl.BlockSpec((ds, nh * hd), lambda s, pt: (s, 0)),
        scratch_shapes=[
            pltpu.VMEM((2, 2 * (past + tail), hd), F32),
            pltpu.SemaphoreType.DMA((2,)),
            pltpu.VMEM((rows, hd), BF16),
            pltpu.VMEM((2, nw, hd), F32),
            pltpu.VMEM((n_groups * ds, chunk), F32),
            pltpu.VMEM((nh, ds, 1), F32),
            pltpu.VMEM((nh, ds, 1), F32),
            pltpu.VMEM((nh, ds, hd), F32),
        ],
    )
    return pl.pallas_call(
        body,
        grid_spec=grid_spec,
        out_shape=jax.ShapeDtypeStruct((n_seq * ds, nh * hd), F32),
        compiler_params=_cparams(("arbitrary",)),
        name="nsa_sample",
    )(page_table, z, z, z, cmp_k, cmp_v, win_buf, cache, bank_c, bank_w, bank_s, bank_t,
      cover, expand)


def _conv_silu(g, g1, g2, u, cw_ref, cb_ref, col0, n_valid):
    cw = cw_ref[...]
    conv = cb_ref[...] + cw[0:1] * g2 + cw[1:2] * g1 + cw[2:3] * g
    act = jax.nn.silu(conv) * u
    col = col0 + lax.broadcasted_iota(I32, act.shape, act.ndim - 1)
    return jnp.where(col < n_valid, act, 0.0)


def _ffn_up_prompt_body(a_ref, wg_ref, wu_ref, cw_ref, cb_ref, act_ref, tail_ref,
                        wg_s, wu_s, carry_s, *, tiles_per_seq, n_valid):
    n = pl.program_id(0)
    m = pl.program_id(1)
    tm, tn = act_ref.shape

    @pl.when(m == 0)
    def _cast():
        wg_s[...] = wg_ref[...].astype(BF16)
        wu_s[...] = wu_ref[...].astype(BF16)

    @pl.when(m % tiles_per_seq == 0)
    def _seq_start():
        carry_s[...] = jnp.zeros_like(carry_s)

    a = a_ref[...]
    g = _dot(a, wg_s[...])
    u = _dot(a, wu_s[...])
    prev = carry_s[...]
    row = lax.broadcasted_iota(I32, (tm, tn), 0)
    last = V7X_SUBLANES - 1
    g1 = jnp.where(row == 0, prev[last:last + 1], pltpu.roll(g, 1, 0))
    g2 = jnp.where(row == 0, prev[last - 1:last],
                   jnp.where(row == 1, prev[last:last + 1], pltpu.roll(g, 2, 0)))
    act_ref[...] = _conv_silu(g, g1, g2, u, cw_ref, cb_ref, n * tn, n_valid).astype(act_ref.dtype)
    carry_s[...] = g[tm - V7X_SUBLANES:tm, :]
    tail_ref[0] = g[tm - V7X_SUBLANES:tm, :]


def _ffn_up_prompt(hn, w_gate, w_up, conv_w, conv_b, *, rows, seq, tm=1024, tn=256):
    d, dff = w_gate.shape
    n_blocks = pl.cdiv(dff, tn)
    assert rows % tm == 0 and seq % tm == 0
    body = functools.partial(_ffn_up_prompt_body, tiles_per_seq=seq // tm, n_valid=dff)
    return pl.pallas_call(
        body,
        grid=(n_blocks, rows // tm),
        in_specs=[
            pl.BlockSpec((tm, d), lambda n, m: (m, 0)),
            pl.BlockSpec((d, tn), lambda n, m: (0, n)),
            pl.BlockSpec((d, tn), lambda n, m: (0, n)),
            pl.BlockSpec((conv_w.shape[0], tn), lambda n, m: (0, n)),
            pl.BlockSpec((1, tn), lambda n, m: (0, n)),
        ],
        out_specs=[
            pl.BlockSpec((tm, tn), lambda n, m: (m, n)),
            pl.BlockSpec((1, V7X_SUBLANES, tn), lambda n, m: (m, 0, n)),
        ],
        out_shape=[
            jax.ShapeDtypeStruct((rows, dff), BF16),
            jax.ShapeDtypeStruct((rows // tm, V7X_SUBLANES, dff), F32),
        ],
        scratch_shapes=[pltpu.VMEM((d, tn), BF16), pltpu.VMEM((d, tn), BF16),
                        pltpu.VMEM((V7X_SUBLANES, tn), F32)],
        compiler_params=_cparams(("arbitrary", "arbitrary")),
        name="ffn_up_prompt",
    )(hn, w_gate, w_up, conv_w, conv_b.reshape(1, dff))


def _ffn_up_sample_body(a_ref, wg_ref, wu_ref, cw_ref, cb_ref, st_ref, act_ref, st_out_ref,
                        *, ds, n_valid):
    n = pl.program_id(0)
    tm, tn = act_ref.shape
    nseq = tm // ds
    a = a_ref[...]
    g = _dot(a, wg_ref[...].astype(BF16)).reshape(nseq, ds, tn)
    u = _dot(a, wu_ref[...].astype(BF16)).reshape(nseq, ds, tn)
    st = st_ref[...]
    s0, s1 = st[:, 0:1, :], st[:, 1:2, :]
    t = lax.broadcasted_iota(I32, (nseq, ds, tn), 1)
    g1 = jnp.where(t == 0, s1, pltpu.roll(g, 1, 1))
    g2 = jnp.where(t == 0, s0, jnp.where(t == 1, s1, pltpu.roll(g, 2, 1)))
    cw = cw_ref[...]
    conv = cb_ref[...] + cw[0:1] * g2 + cw[1:2] * g1 + cw[2:3] * g
    act = jax.nn.silu(conv) * u
    col = n * tn + lax.broadcasted_iota(I32, act.shape, 2)
    act = jnp.where(col < n_valid, act, 0.0)
    act_ref[...] = act.reshape(tm, tn).astype(act_ref.dtype)
    st_out_ref[...] = g[:, ds - 2:ds, :]


def _ffn_up_sample(hn, w_gate, w_up, conv_w, conv_b, state, *, row0, rows, ds, tn=256):
    d, dff = w_gate.shape
    n_blocks = pl.cdiv(dff, tn)
    assert row0 % rows == 0 and rows % ds == 0 and ds == V7X_SUBLANES
    nseq = rows // ds
    r0 = row0 // rows
    body = functools.partial(_ffn_up_sample_body, ds=ds, n_valid=dff)
    return pl.pallas_call(
        body,
        grid=(n_blocks,),
        in_specs=[
            pl.BlockSpec((rows, d), lambda n: (r0, 0)),
            pl.BlockSpec((d, tn), lambda n: (0, n)),
            pl.BlockSpec((d, tn), lambda n: (0, n)),
            pl.BlockSpec((conv_w.shape[0], tn), lambda n: (0, n)),
            pl.BlockSpec((1, tn), lambda n: (0, n)),
            pl.BlockSpec((nseq, state.shape[1], tn), lambda n: (0, 0, n)),
        ],
        out_specs=[
            pl.BlockSpec((rows, tn), lambda n: (0, n)),
            pl.BlockSpec((nseq, state.shape[1], tn), lambda n: (0, 0, n)),
        ],
        out_shape=[
            jax.ShapeDtypeStruct((rows, dff), BF16),
            jax.ShapeDtypeStruct(state.shape, F32),
        ],
        compiler_params=_cparams(("parallel",)),
        name="ffn_up_sample",
    )(hn, w_gate, w_up, conv_w, conv_b.reshape(1, dff), state)


def _layout_w_in(w_in):
    d = w_in.shape[0]
    nq = NSA_HEADS * NSA_HEAD_DIM
    nkv = 6 * NSA_HEAD_DIM
    ng = 3 * NSA_HEADS
    o_kv, o_g = nq, nq + nkv
    o_qa = o_g + ng
    o_kva = o_qa + MLA_Q_LORA
    o_m = o_kva + MLA_KV_LORA + MLA_ROPE
    d_model = (w_in.shape[1] - o_m) // 2

    def zeros(n):
        return jnp.zeros((d, n), w_in.dtype)

    parts = [
        w_in[:, 0:nq],
        w_in[:, o_g:o_g + ng], zeros(Z_QA - Z_G - ng),
        w_in[:, o_qa:o_qa + MLA_Q_LORA],
        w_in[:, o_kv:o_kv + nkv],
        w_in[:, o_kva + MLA_KV_LORA:o_kva + MLA_KV_LORA + MLA_ROPE],
        zeros(Z_CKV - Z_KPE - MLA_ROPE),
        w_in[:, o_kva:o_kva + MLA_KV_LORA],
        w_in[:, o_m:o_m + 2 * d_model],
    ]
    out = jnp.concatenate(parts, axis=1).astype(BF16)
    assert out.shape[1] == Z_WIDTH and Z_MB - Z_MA == d_model
    return out


def _rope_tables(pos):
    half = MLA_ROPE // 2
    inv = ROPE_THETA ** (-jnp.arange(half, dtype=F32) / half)
    ang = pos.astype(F32)[:, None] * inv
    cos, sin = jnp.cos(ang), jnp.sin(ang)
    pad = jnp.zeros((pos.shape[0], V7X_LANES - MLA_ROPE), F32)
    return (jnp.concatenate([cos, cos, pad], axis=1), jnp.concatenate([-sin, sin, pad], axis=1))


def kernel(x_prompt, x_sample, cache_mla, cache_nsa_cmp_kv, cache_nsa_slc_kv, cache_nsa_win_kv,
           state_ffn_conv, page_table, rel_bias, norm_attn, w_in, nsa_phi_pos, nsa_phi_w1, nsa_phi_w2,
           mla_q_norm, mla_w_uq, mla_kv_norm, mla_w_ukv, w_br_nsa, w_br_mla, w_out, norm_ffn,
           w_ffn_gate, w_ffn_up, ffn_conv_w, ffn_conv_b, w_ffn_down, norm_final):
    depth = norm_attn.shape[0]
    assert depth == 1, "the layer loop is written for the single-layer configuration"
    lyr = 0
    b, seq, d = x_prompt.shape
    db, ds, _ = x_sample.shape
    mp, msm = b * seq, db * ds
    m = mp + msm
    n_pool, page = cache_mla.shape[1], cache_mla.shape[2]
    n_pages = page_table.shape[1]
    past = n_pages * page
    hd = NSA_HEAD_DIM
    nh = NSA_HEADS
    assert seq % Q_BLOCK == 0 and ds == V7X_SUBLANES and past >= WINDOW
    assert cache_nsa_win_kv.shape[2] == WINDOW and seq >= WINDOW

    x_all = jnp.concatenate([x_prompt.reshape(mp, d), x_sample.reshape(msm, d)], axis=0)
    xn = _rmsnorm(x_all, norm_attn[lyr], BF16, name="norm_attn")
    z = _matmul(xn, _layout_w_in(w_in[lyr]), tm=1024, tn=1280, tk=2048, out_dtype=F32, name="in_proj")

    pos_all = jnp.concatenate([jnp.tile(jnp.arange(seq, dtype=I32), b),
                               jnp.tile(past + jnp.arange(ds, dtype=I32), db)])
    cs, sn = _rope_tables(pos_all)
    w_uq = mla_w_uq[lyr].reshape(MLA_Q_LORA, MLA_HEADS, MLA_NOPE + MLA_ROPE)
    w_uq_p = jnp.concatenate([
        w_uq[:, :, :MLA_NOPE].reshape(MLA_Q_LORA, MLA_HEADS * MLA_NOPE),
        jnp.pad(w_uq[:, :, MLA_NOPE:], ((0, 0), (0, 0), (0, V7X_LANES - MLA_ROPE))).reshape(
            MLA_Q_LORA, MLA_HEADS * V7X_LANES)], axis=1).astype(BF16)
    w_ukv = mla_w_ukv[lyr]
    wk = jnp.transpose(w_ukv[:, :, :MLA_NOPE], (1, 2, 0)).astype(BF16)
    w_uv = jnp.transpose(w_ukv[:, :, MLA_NOPE:], (1, 0, 2)).astype(BF16)
    q_lat, q_pe, lat = _mla_project(z, cs, sn, mla_q_norm[lyr], w_uq_p, mla_kv_norm[lyr], wk)

    pt = page_table + lyr * n_pool
    lat_w = MLA_KV_LORA + MLA_ROPE
    o_lat_p = _mla_prompt(q_lat, q_pe, lat, batch=b, seq=seq)
    cache_mla_t = jnp.swapaxes(cache_mla, 2, 3).reshape(depth * n_pool, lat_w, page)
    o_lat_s = _mla_sample(pt, q_lat, q_pe, lat, cache_mla_t, row0=mp, ds=ds)
    o_mla = jnp.concatenate([_mla_uv(o_lat_p, w_uv, name="mla_uv_prompt"),
                             _mla_uv(o_lat_s, w_uv, name="mla_uv_sample")], axis=0)

    pad = WINDOW + Q_BLOCK
    i_q = np.arange(Q_BLOCK)[:, None]
    bank = _bias_bank(rel_bias, _rel_bucket_np(i_q + pad - np.arange(8 * V7X_LANES)[None, :]), "bias_bank_prompt")
    bank = jnp.transpose(bank.reshape(nh, Q_BLOCK, 8, V7X_LANES), (2, 0, 1, 3)).reshape(
        8, nh * Q_BLOCK, V7X_LANES)
    nc_p = seq // CMP_STRIDE
    qpos_p = np.arange(seq)[:, None]
    bank_cp = _bias_bank(rel_bias, _rel_bucket_np(
        qpos_p - (np.arange(nc_p)[None, :] * CMP_STRIDE + CMP_BLOCK - 1)), "bias_bank_prompt_cmp")
    ns_p = -(-seq // SLC_BLOCK)

    def cover_matrix(n_cmp, n_cmp_pad, n_slc, n_slc_pad):
        c_start = np.arange(n_cmp_pad)[:, None] * CMP_STRIDE
        s_start = np.arange(n_slc_pad)[None, :] * SLC_BLOCK
        cov = (c_start < s_start + SLC_BLOCK) & (c_start + CMP_BLOCK > s_start)
        cov &= (np.arange(n_cmp_pad)[:, None] < n_cmp) & (np.arange(n_slc_pad)[None, :] < n_slc)
        return jnp.asarray(cov.astype(np.float32), dtype=BF16)

    cover_p = cover_matrix(seq // CMP_STRIDE - 1, nc_p, ns_p, V7X_LANES)
    expand_p = jnp.asarray(
        (np.arange(seq)[None, :] // SLC_BLOCK == np.arange(V7X_LANES)[:, None]).astype(np.float32), dtype=BF16)
    phi = nsa_phi_pos[lyr].reshape(CMP_BLOCK, 2 * hd)
    half = CMP_STRIDE * hd
    w1 = jnp.concatenate([nsa_phi_w1[lyr][:, :half], nsa_phi_w1[lyr][:, half:]], axis=2).astype(BF16)
    w2 = nsa_phi_w2[lyr]
    cmp_k, cmp_v = _cmp_prompt(z, phi, w1, w2, batch=b, seq=seq)
    win_new = z[:, Z_KV + 4 * hd:Z_KV + 6 * hd]
    tk_nsa = 512
    win_pad = jnp.pad(win_new[:mp].reshape(b, seq, 2 * hd),
                      ((0, 0), (pad, -(pad + Q_BLOCK) % tk_nsa), (0, 0)))
    o_nsa_p = _nsa_prompt(z, cmp_k, cmp_v, win_pad, bank, bank_cp, cover_p, expand_p, batch=b, seq=seq,
                          tk=tk_nsa)

    cache_cmp = cache_nsa_cmp_kv.reshape(depth * n_pool, 2 * page, hd)
    cache_slc = cache_nsa_slc_kv.reshape(depth * n_pool, 2 * page, hd)
    cmp_ks, cmp_vs = _cmp_sample(pt, z, cache_cmp, phi, w1, w2, row0=mp, ds=ds)
    nc_s = cmp_ks.shape[1]
    chunk = 2048
    t_q = np.arange(ds)[:, None]
    wb = cache_nsa_win_kv.shape[2]
    nw = -(-(wb + ds) // V7X_LANES) * V7X_LANES

    def sample_bank(dist, name):
        return _bias_bank(rel_bias, _rel_bucket_np(dist), name).reshape(nh * ds, dist.shape[1])

    bank_cs = sample_bank(past + t_q - (np.arange(nc_s)[None, :] * CMP_STRIDE + CMP_BLOCK - 1), "bias_bank_sample_cmp")
    bank_ws = sample_bank(t_q + wb - np.arange(nw)[None, :], "bias_bank_sample_win")
    bank_ss = sample_bank(past + t_q - (past - chunk + np.arange(chunk)[None, :]), "bias_bank_sample_slc")
    bank_ts = sample_bank(t_q - np.arange(V7X_LANES)[None, :], "bias_bank_sample_new")
    ns_s = -(-(past + ds) // SLC_BLOCK)
    ns_s_pad = -(-ns_s // V7X_LANES) * V7X_LANES
    n_cmp_s = -(-(past + ds) // CMP_STRIDE) - CMP_BLOCK // CMP_STRIDE + 1
    cover_s = cover_matrix(n_cmp_s, nc_s, ns_s, ns_s_pad)
    expand_s = jnp.asarray(
        (np.arange(chunk)[None, :] // SLC_BLOCK == np.arange(V7X_LANES)[:, None]).astype(np.float32), dtype=BF16)
    o_nsa_s = _nsa_sample(pt, z, cmp_ks, cmp_vs, cache_nsa_win_kv[lyr].reshape(db, 2 * wb, hd), cache_slc,
                          bank_cs, bank_ws, bank_ss, bank_ts, cover_s, expand_s,
                          row0=mp, ds=ds, chunk=chunk)
    o_nsa = jnp.concatenate([o_nsa_p, o_nsa_s], axis=0)

    t1 = _matmul(o_nsa, w_br_nsa[lyr], tm=1024, tn=512, tk=2048, out_dtype=F32, extras=[(z, Z_MA)],
                 epilogue=lambda acc, zm: jax.nn.sigmoid(zm) * acc, name="merge_nsa")
    y = _matmul(o_mla, w_br_mla[lyr], tm=1024, tn=512, tk=2048, out_dtype=BF16,
                extras=[(z, Z_MB), (t1, 0)],
                epilogue=lambda acc, zm, t: t + jax.nn.sigmoid(zm) * acc, name="merge_mla")
    h1 = _matmul(y, w_out[lyr], tm=1024, tn=1024, tk=2048, out_dtype=F32, extras=[(x_all, 0)],
                 epilogue=lambda acc, x: x + acc, name="out_proj")

    hn = _rmsnorm(h1, norm_ffn[lyr], BF16, name="norm_ffn")
    dff = w_ffn_gate.shape[2]
    assert ffn_conv_w.shape[1] == 3 and state_ffn_conv.shape[2] == 2
    tm_f = 1024
    act_p, tails = _ffn_up_prompt(hn, w_ffn_gate[lyr], w_ffn_up[lyr], ffn_conv_w[lyr], ffn_conv_b[lyr],
                                  rows=mp, seq=seq, tm=tm_f)
    act_s, conv_s = _ffn_up_sample(hn, w_ffn_gate[lyr], w_ffn_up[lyr], ffn_conv_w[lyr], ffn_conv_b[lyr],
                                   state_ffn_conv[lyr], row0=mp, rows=msm, ds=ds)
    add = lambda acc, h: h + acc
    h2_p = _matmul(act_p, w_ffn_down[lyr], tm=1024, tn=1024, tk=1024, out_dtype=F32, extras=[(h1, 0)],
                   epilogue=add, name="ffn_down_prompt")
    h2_s = _matmul(act_s, w_ffn_down[lyr], tm=1024, tn=1024, tk=1024, out_dtype=F32, extras=[(h1, 0, mp)],
                   epilogue=add, name="ffn_down_sample")
    y_p = _rmsnorm(h2_p, norm_final, F32, name="norm_final_prompt").reshape(b, seq, d)
    y_s = _rmsnorm(h2_s, norm_final, F32, name="norm_final_sample").reshape(db, ds, d)

    kv_tail = (2, 1, hd)
    kv_new = z[:, Z_KV:Z_KV + 6 * hd]
    kv_p = kv_new[:mp].reshape(b, seq, 3, *kv_tail)
    kv_s = kv_new[mp:].reshape(db, ds, 3, *kv_tail)
    win_all = jnp.concatenate([cache_nsa_win_kv[lyr], kv_s[:, :, 2]], axis=1)
    tps = seq // tm_f
    conv_p = tails.reshape(b, tps, V7X_SUBLANES, dff)[:, tps - 1, V7X_SUBLANES - 2:, :]
    return (y_p, y_s,
            lat[:mp].reshape(1, b, seq, lat_w), lat[mp:].reshape(1, db, ds, lat_w),
            kv_p[None, :, :, 0], kv_s[None, :, :, 0],
            kv_p[None, :, :, 1], kv_s[None, :, :, 1],
            kv_p[None, :, seq - min(WINDOW, seq):, 2],
            win_all[None, :, wb + ds - min(WINDOW, wb + ds):],
            conv_p[None], conv_s[None])
```

```python
import functools
import math

import numpy as np
import jax
import jax.numpy as jnp
from jax import lax
from jax.experimental import pallas as pl
from jax.experimental.pallas import tpu as pltpu

F32, BF16, I32 = jnp.float32, jnp.bfloat16, jnp.int32

V7X_LANES = 128
V7X_SUBLANES = 8
V7X_VMEM_BYTES = 64 * 1024 * 1024
VMEM_LIMIT = V7X_VMEM_BYTES - 8 * 1024 * 1024

NSA_HEADS = 16
NSA_HEAD_DIM = 128
CMP_BLOCK = 32
CMP_STRIDE = 16
SLC_BLOCK = 64
N_SELECT = 16
WINDOW = 512
MLA_HEADS = 16
MLA_Q_LORA = 768
MLA_KV_LORA = 512
MLA_NOPE = 128
MLA_ROPE = 64
MLA_V = 128
MLA_SCALE = (MLA_NOPE + MLA_ROPE) ** -0.5
ROPE_THETA = 10000.0
REL_BUCKETS = 32
REL_MAX_DIST = 128
RMS_EPS = 1e-6
Q_BLOCK = 128
NEG_INF = -1e30
FORCED_SCORE = 1e30
PAD_SCORE = -3e38

Z_Q = 0
Z_G = 2048
Z_QA = 2304
Z_KV = 3072
Z_KPE = 3840
Z_CKV = 4096
Z_MA = 4608
Z_MB = 8704
Z_WIDTH = 12800


def _cparams(sem, vmem=VMEM_LIMIT):
    return pltpu.CompilerParams(dimension_semantics=sem, vmem_limit_bytes=vmem)


def _dot(a, b):
    return jnp.dot(a, b, preferred_element_type=F32)


def _dot_nt(a, b):
    return lax.dot_general(a, b, (((1,), (1,)), ((), ())), preferred_element_type=F32)


def _split_bf16(x):
    hi = x.astype(BF16)
    lo = (x - hi.astype(F32)).astype(BF16)
    return hi, lo


def _rmsnorm_body(x_ref, g_ref, o_ref):
    x = x_ref[...]
    ms = jnp.mean(x * x, axis=-1, keepdims=True)
    o_ref[...] = (x * lax.rsqrt(ms + RMS_EPS) * g_ref[...]).astype(o_ref.dtype)


def _rmsnorm(x, g, out_dtype, *, row0=0, rows=None, tm=256, name="rmsnorm"):
    m_all, d = x.shape
    rows = m_all if rows is None else rows
    assert rows % tm == 0 and row0 % tm == 0
    b0 = row0 // tm
    return pl.pallas_call(
        _rmsnorm_body,
        grid=(rows // tm,),
        in_specs=[pl.BlockSpec((tm, d), lambda i: (b0 + i, 0)),
                  pl.BlockSpec((1, d), lambda i: (0, 0))],
        out_specs=pl.BlockSpec((tm, d), lambda i: (i, 0)),
        out_shape=jax.ShapeDtypeStruct((rows, d), out_dtype),
        compiler_params=_cparams(("parallel",)),
        name=name,
    )(x, g.reshape(1, d))


def _mm_body(*refs, nk, tk, k_valid, n_extra, epilogue):
    a_ref, w_ref = refs[0], refs[1]
    extra = refs[2:2 + n_extra]
    o_ref, acc_ref = refs[2 + n_extra], refs[3 + n_extra]
    k = pl.program_id(2)

    @pl.when(k == 0)
    def _init():
        acc_ref[...] = jnp.zeros_like(acc_ref)

    a = a_ref[...].astype(BF16)
    w = w_ref[...].astype(BF16)
    if k_valid % tk:
        row = lax.broadcasted_iota(I32, w.shape, 0) + k * tk
        w = jnp.where(row < k_valid, w, jnp.zeros_like(w))
        col = lax.broadcasted_iota(I32, a.shape, 1) + k * tk
        a = jnp.where(col < k_valid, a, jnp.zeros_like(a))
    acc_ref[...] += _dot(a, w)

    @pl.when(k == nk - 1)
    def _fin():
        o_ref[...] = epilogue(acc_ref[...], *(e[...] for e in extra)).astype(o_ref.dtype)


def _matmul(a, w, *, tm, tn, tk, out_dtype, a_col0=0, extras=(), epilogue=None, name):
    m = a.shape[0]
    k_valid, n = w.shape
    nk = pl.cdiv(k_valid, tk)
    assert m % tm == 0 and n % tn == 0 and a_col0 % tk == 0
    assert a.shape[1] >= a_col0 + k_valid
    if epilogue is None:
        epilogue = lambda acc: acc
    ka0 = a_col0 // tk
    in_specs = [pl.BlockSpec((tm, tk), lambda i, j, k: (i, ka0 + k)),
                pl.BlockSpec((tk, tn), lambda i, j, k: (k, j))]
    args = [a, w]
    for arr, col0, *rest in extras:
        row0 = rest[0] if rest else 0
        assert col0 % tn == 0 and row0 % tm == 0
        in_specs.append(pl.BlockSpec((tm, tn), lambda i, j, k, c=col0 // tn, r=row0 // tm: (r + i, c + j)))
        args.append(arr)
    body = functools.partial(_mm_body, nk=nk, tk=tk, k_valid=k_valid, n_extra=len(extras),
                             epilogue=epilogue)
    return pl.pallas_call(
        body,
        grid=(m // tm, n // tn, nk),
        in_specs=in_specs,
        out_specs=pl.BlockSpec((tm, tn), lambda i, j, k: (i, j)),
        out_shape=jax.ShapeDtypeStruct((m, n), out_dtype),
        scratch_shapes=[pltpu.VMEM((tm, tn), F32)],
        compiler_params=_cparams(("parallel", "parallel", "arbitrary")),
        name=name,
    )(*args)


def _rope_tile(x, cs, sn):
    lane = lax.broadcasted_iota(I32, x.shape, x.ndim - 1) & (V7X_LANES - 1)
    half = MLA_ROPE // 2
    rot = jnp.where(lane < half, pltpu.roll(x, V7X_LANES - half, x.ndim - 1),
                    pltpu.roll(x, half, x.ndim - 1))
    return x * cs + rot * sn


def _mla_proj_body(qa_ref, kpe_ref, ckv_ref, cs_ref, sn_ref, qn_ref, wuq_ref, kvn_ref, wk_ref,
                   qlat_ref, qpe_ref, lat_ref):
    qa = qa_ref[...]
    ms = jnp.mean(qa * qa, axis=-1, keepdims=True)
    qan = (qa * lax.rsqrt(ms + RMS_EPS) * qn_ref[...]).astype(BF16)
    q = _dot(qan, wuq_ref[...])
    nope_w = MLA_HEADS * MLA_NOPE
    for h in range(MLA_HEADS):
        qn_h = q[:, h * MLA_NOPE:(h + 1) * MLA_NOPE].astype(BF16)
        qlat_ref[:, h * MLA_KV_LORA:(h + 1) * MLA_KV_LORA] = _dot(qn_h, wk_ref[h])
    cs = cs_ref[...]
    sn = sn_ref[...]
    for h in range(MLA_HEADS):
        sl = slice(h * V7X_LANES, (h + 1) * V7X_LANES)
        qpe_ref[:, sl] = _rope_tile(q[:, nope_w + h * V7X_LANES:nope_w + (h + 1) * V7X_LANES], cs, sn)
    ckv = ckv_ref[...]
    ms2 = jnp.mean(ckv * ckv, axis=-1, keepdims=True)
    lat_ref[:, 0:MLA_KV_LORA] = ckv * lax.rsqrt(ms2 + RMS_EPS) * kvn_ref[...]
    kpe = _rope_tile(kpe_ref[...], cs, sn)
    lat_ref[:, MLA_KV_LORA:MLA_KV_LORA + MLA_ROPE] = kpe[:, 0:MLA_ROPE]


def _mla_project(z, cs, sn, q_norm, w_uq_p, kv_norm, wk, *, tm=256):
    m = z.shape[0]
    lat_w = MLA_KV_LORA + MLA_ROPE
    return pl.pallas_call(
        _mla_proj_body,
        grid=(m // tm,),
        in_specs=[
            pl.BlockSpec((tm, MLA_Q_LORA), lambda i: (i, Z_QA // MLA_Q_LORA)),
            pl.BlockSpec((tm, V7X_LANES), lambda i: (i, Z_KPE // V7X_LANES)),
            pl.BlockSpec((tm, MLA_KV_LORA), lambda i: (i, Z_CKV // MLA_KV_LORA)),
            pl.BlockSpec((tm, V7X_LANES), lambda i: (i, 0)),
            pl.BlockSpec((tm, V7X_LANES), lambda i: (i, 0)),
            pl.BlockSpec((1, MLA_Q_LORA), lambda i: (0, 0)),
            pl.BlockSpec(w_uq_p.shape, lambda i: (0, 0)),
            pl.BlockSpec((1, MLA_KV_LORA), lambda i: (0, 0)),
            pl.BlockSpec(wk.shape, lambda i: (0, 0, 0)),
        ],
        out_specs=[
            pl.BlockSpec((tm, MLA_HEADS * MLA_KV_LORA), lambda i: (i, 0)),
            pl.BlockSpec((tm, MLA_HEADS * V7X_LANES), lambda i: (i, 0)),
            pl.BlockSpec((tm, lat_w), lambda i: (i, 0)),
        ],
        out_shape=[
            jax.ShapeDtypeStruct((m, MLA_HEADS * MLA_KV_LORA), F32),
            jax.ShapeDtypeStruct((m, MLA_HEADS * V7X_LANES), F32),
            jax.ShapeDtypeStruct((m, lat_w), F32),
        ],
        compiler_params=_cparams(("parallel",)),
        name="mla_project",
    )(z, z, z, cs, sn, q_norm.reshape(1, -1), w_uq_p, kv_norm.reshape(1, -1), wk)


def _stack_heads(ref, width, valid=None):
    valid = width if valid is None else valid
    return jnp.concatenate(
        [ref[:, h * width:h * width + valid].astype(BF16) for h in range(MLA_HEADS)], axis=0)


def _mla_prompt_body(qlat_ref, qpe_ref, lat_ref, o_ref, qlat_s, qpe_s, m_s, l_s, acc_s, *, tk):
    qb = pl.program_id(1)
    nq = qlat_ref.shape[0]
    rows = MLA_HEADS * nq
    qlat_s[...] = _stack_heads(qlat_ref, MLA_KV_LORA)
    qpe_s[...] = _stack_heads(qpe_ref, V7X_LANES, MLA_ROPE)
    m_s[...] = jnp.full(m_s.shape, NEG_INF, F32)
    l_s[...] = jnp.zeros(l_s.shape, F32)
    acc_s[...] = jnp.zeros(acc_s.shape, F32)
    qpos = qb * nq + (lax.broadcasted_iota(I32, (rows, tk), 0) & (nq - 1))
    lane = lax.broadcasted_iota(I32, (rows, tk), 1)

    def step(kt, masked):
        k0 = pl.multiple_of(kt * tk, tk)
        kv = lat_ref[pl.ds(k0, tk), :]
        ckv = kv[:, 0:MLA_KV_LORA].astype(BF16)
        kpe = kv[:, MLA_KV_LORA:MLA_KV_LORA + MLA_ROPE].astype(BF16)
        s = (_dot_nt(qlat_s[...], ckv) + _dot_nt(qpe_s[...], kpe)) * MLA_SCALE
        if masked:
            valid = (k0 + lane) <= qpos
            s = jnp.where(valid, s, NEG_INF)
        m_prev = m_s[...]
        m_new = jnp.maximum(m_prev, jnp.max(s, axis=1, keepdims=True))
        alpha = jnp.exp(m_prev - m_new)
        p = jnp.exp(s - m_new)
        if masked:
            p = jnp.where(valid, p, 0.0)
        l_s[...] = alpha * l_s[...] + jnp.sum(p, axis=1, keepdims=True)
        acc_s[...] = alpha * acc_s[...] + _dot(p.astype(BF16), ckv)
        m_s[...] = m_new

    def full_step(kt, carry):
        step(kt, False)
        return carry

    def diag_step(kt, carry):
        step(kt, True)
        return carry

    n_full = (qb * nq) // tk
    n_tiles = ((qb + 1) * nq + tk - 1) // tk
    lax.fori_loop(0, n_full, full_step, 0)
    lax.fori_loop(n_full, n_tiles, diag_step, 0)
    o = acc_s[...] / l_s[...]
    for h in range(MLA_HEADS):
        o_ref[:, h * MLA_KV_LORA:(h + 1) * MLA_KV_LORA] = o[h * nq:(h + 1) * nq, :]


def _mla_prompt(q_lat, q_pe, lat, *, batch, seq, tk=512):
    nqb = seq // Q_BLOCK
    rows = MLA_HEADS * Q_BLOCK
    lat_w = MLA_KV_LORA + MLA_ROPE
    return pl.pallas_call(
        functools.partial(_mla_prompt_body, tk=tk),
        grid=(batch, nqb),
        in_specs=[
            pl.BlockSpec((Q_BLOCK, MLA_HEADS * MLA_KV_LORA), lambda b, q: (b * nqb + q, 0)),
            pl.BlockSpec((Q_BLOCK, MLA_HEADS * V7X_LANES), lambda b, q: (b * nqb + q, 0)),
            pl.BlockSpec((seq, lat_w), lambda b, q: (b, 0)),
        ],
        out_specs=pl.BlockSpec((Q_BLOCK, MLA_HEADS * MLA_KV_LORA), lambda b, q: (b * nqb + q, 0)),
        out_shape=jax.ShapeDtypeStruct((batch * seq, MLA_HEADS * MLA_KV_LORA), F32),
        scratch_shapes=[
            pltpu.VMEM((rows, MLA_KV_LORA), BF16),
            pltpu.VMEM((rows, MLA_ROPE), BF16),
            pltpu.VMEM((rows, 1), F32),
            pltpu.VMEM((rows, 1), F32),
            pltpu.VMEM((rows, MLA_KV_LORA), F32),
        ],
        compiler_params=_cparams(("parallel", "arbitrary")),
        name="mla_prompt",
    )(q_lat, q_pe, lat)


def _page_gather(pt_ref, cache_ref, dst_of, sem, seq, page0, n_pages, start, unroll=1):
    def body(g, carry):
        for u in range(unroll):
            src = cache_ref.at[pt_ref[seq, page0 + g * unroll + u]] if start else cache_ref.at[0]
            cp = pltpu.make_async_copy(src, dst_of(g, u), sem)
            if start:
                cp.start()
            else:
                cp.wait()
        return carry
    lax.fori_loop(0, n_pages // unroll, body, 0)


def _row_pages(buf, slot, rows_per_page):
    return lambda g, u: buf.at[slot, pl.ds((g + u) * rows_per_page, rows_per_page)]


def _mla_sample_body(pt_ref, qlat_ref, qpe_ref, new_ref, cache_ref, o_ref,
                     buf, sem, qlat_s, qpe_s, new_s, m_s, l_s, acc_s,
                     *, n_seq, n_chunks, ppc, page, sub):
    s_id = pl.program_id(0)
    c_id = pl.program_id(1)
    g = s_id * n_chunks + c_id
    slot = g % 2
    ds = qlat_ref.shape[0]
    rows = MLA_HEADS * ds
    pps = sub // page

    def dst_of(slot_):
        return lambda gi, u: buf.at[slot_, gi, :, u * page:(u + 1) * page]

    @pl.when(g == 0)
    def _first():
        _page_gather(pt_ref, cache_ref, dst_of(0), sem.at[0], 0, 0, ppc, True, unroll=pps)

    @pl.when(g + 1 < n_seq * n_chunks)
    def _prefetch():
        g1 = g + 1
        _page_gather(pt_ref, cache_ref, dst_of(g1 % 2), sem.at[g1 % 2], g1 // n_chunks,
                     (g1 % n_chunks) * ppc, ppc, True, unroll=pps)

    @pl.when(c_id == 0)
    def _init():
        qlat_s[...] = _stack_heads(qlat_ref, MLA_KV_LORA)
        qpe_s[...] = _stack_heads(qpe_ref, V7X_LANES, MLA_ROPE)
        m_s[...] = jnp.full(m_s.shape, NEG_INF, F32)
        l_s[...] = jnp.zeros(l_s.shape, F32)
        acc_s[...] = jnp.zeros(acc_s.shape, F32)

    _page_gather(pt_ref, cache_ref, dst_of(slot), sem.at[slot], s_id, 0, ppc, False, unroll=pps)

    def flash(s, valid, pv_of):
        s = s * MLA_SCALE
        if valid is not None:
            s = jnp.where(valid, s, NEG_INF)
        m_prev = m_s[...]
        m_new = jnp.maximum(m_prev, jnp.max(s, axis=1, keepdims=True))
        alpha = jnp.exp(m_prev - m_new)
        p = jnp.exp(s - m_new)
        if valid is not None:
            p = jnp.where(valid, p, 0.0)
        l_s[...] = alpha * l_s[...] + jnp.sum(p, axis=1, keepdims=True)
        acc_s[...] = alpha * acc_s[...] + pv_of(p.astype(BF16))
        m_s[...] = m_new

    def step(j, carry):
        kt = buf[slot, j]
        ckv_t = kt[0:MLA_KV_LORA, :].astype(BF16)
        kpe_t = kt[MLA_KV_LORA:MLA_KV_LORA + MLA_ROPE, :].astype(BF16)
        s = _dot(qlat_s[...], ckv_t) + _dot(qpe_s[...], kpe_t)
        flash(s, None, lambda p: _dot_nt(p, ckv_t))
        return carry

    lax.fori_loop(0, (ppc * page) // sub, step, 0, unroll=True)

    @pl.when(c_id == n_chunks - 1)
    def _finish():
        new_s[...] = jnp.zeros(new_s.shape, F32)
        new_s[0:ds, :] = new_ref[...]
        nk = new_s.shape[0]
        t_q = lax.broadcasted_iota(I32, (rows, nk), 0) & (ds - 1)
        j_k = lax.broadcasted_iota(I32, (rows, nk), 1)
        kv = new_s[...]
        ckv = kv[:, 0:MLA_KV_LORA].astype(BF16)
        kpe = kv[:, MLA_KV_LORA:MLA_KV_LORA + MLA_ROPE].astype(BF16)
        s = _dot_nt(qlat_s[...], ckv) + _dot_nt(qpe_s[...], kpe)
        flash(s, j_k <= t_q, lambda p: _dot(p, ckv))
        o = acc_s[...] / l_s[...]
        for h in range(MLA_HEADS):
            o_ref[:, h * MLA_KV_LORA:(h + 1) * MLA_KV_LORA] = o[h * ds:(h + 1) * ds, :]


def _mla_sample(page_table, q_lat, q_pe, lat, cache_t, *, row0, ds, ppc=32, sub=2048):
    n_seq, n_pages = page_table.shape
    lat_w, page = cache_t.shape[1], cache_t.shape[2]
    assert n_pages % ppc == 0 and (ppc * page) % sub == 0 and sub % page == 0 and row0 % ds == 0
    n_chunks = n_pages // ppc
    rows = MLA_HEADS * ds
    r0 = row0 // ds
    body = functools.partial(_mla_sample_body, n_seq=n_seq, n_chunks=n_chunks, ppc=ppc, page=page,
                             sub=sub)
    grid_spec = pltpu.PrefetchScalarGridSpec(
        num_scalar_prefetch=1,
        grid=(n_seq, n_chunks),
        in_specs=[
            pl.BlockSpec((ds, MLA_HEADS * MLA_KV_LORA), lambda s, c, pt: (r0 + s, 0)),
            pl.BlockSpec((ds, MLA_HEADS * V7X_LANES), lambda s, c, pt: (r0 + s, 0)),
            pl.BlockSpec((ds, lat_w), lambda s, c, pt: (r0 + s, 0)),
            pl.BlockSpec(memory_space=pl.ANY),
        ],
        out_specs=pl.BlockSpec((ds, MLA_HEADS * MLA_KV_LORA), lambda s, c, pt: (s, 0)),
        scratch_shapes=[
            pltpu.VMEM((2, (ppc * page) // sub, lat_w, sub), F32),
            pltpu.SemaphoreType.DMA((2,)),
            pltpu.VMEM((rows, MLA_KV_LORA), BF16),
            pltpu.VMEM((rows, MLA_ROPE), BF16),
            pltpu.VMEM((V7X_LANES, lat_w), F32),
            pltpu.VMEM((rows, 1), F32),
            pltpu.VMEM((rows, 1), F32),
            pltpu.VMEM((rows, MLA_KV_LORA), F32),
        ],
    )
    return pl.pallas_call(
        body,
        grid_spec=grid_spec,
        out_shape=jax.ShapeDtypeStruct((n_seq * ds, MLA_HEADS * MLA_KV_LORA), F32),
        compiler_params=_cparams(("arbitrary", "arbitrary")),
        name="mla_sample",
    )(page_table, q_lat, q_pe, lat, cache_t)


def _mla_uv_body(o_ref, w_ref, out_ref):
    for h in range(MLA_HEADS):
        o_h = o_ref[:, h * MLA_KV_LORA:(h + 1) * MLA_KV_LORA].astype(BF16)
        out_ref[:, h * MLA_V:(h + 1) * MLA_V] = _dot(o_h, w_ref[h]).astype(out_ref.dtype)


def _mla_uv(o_lat, w_uv, *, tm=256, name="mla_uv"):
    m = o_lat.shape[0]
    assert m % tm == 0
    return pl.pallas_call(
        _mla_uv_body,
        grid=(m // tm,),
        in_specs=[pl.BlockSpec((tm, MLA_HEADS * MLA_KV_LORA), lambda i: (i, 0)),
                  pl.BlockSpec(w_uv.shape, lambda i: (0, 0, 0))],
        out_specs=pl.BlockSpec((tm, MLA_HEADS * MLA_V), lambda i: (i, 0)),
        out_shape=jax.ShapeDtypeStruct((m, MLA_HEADS * MLA_V), BF16),
        compiler_params=_cparams(("parallel",)),
        name=name,
    )(o_lat, w_uv)


def _rel_bucket_np(dist):
    n = np.maximum(dist, 0)
    max_exact = REL_BUCKETS // 2
    nf = np.maximum(n, max_exact).astype(np.float32)
    large = max_exact + (np.log(nf / max_exact) / math.log(REL_MAX_DIST / max_exact)
                         * (REL_BUCKETS - max_exact)).astype(np.int32)
    large = np.minimum(large, REL_BUCKETS - 1)
    return np.where(n < max_exact, n, large).astype(np.int32)


def _bias_bank_body(rb_ref, bucket_ref, o_ref):
    h = pl.program_id(0)
    bucket = bucket_ref[...]
    acc = jnp.zeros(bucket.shape, F32)
    for b in range(REL_BUCKETS):
        acc = jnp.where(bucket == b, rb_ref[b, h], acc)
    o_ref[0] = acc


def _bias_bank(rel_bias, bucket, name):
    r, c = bucket.shape
    return pl.pallas_call(
        _bias_bank_body,
        grid=(NSA_HEADS,),
        in_specs=[pl.BlockSpec(memory_space=pltpu.SMEM),
                  pl.BlockSpec((r, c), lambda h: (0, 0))],
        out_specs=pl.BlockSpec((1, r, c), lambda h: (h, 0, 0)),
        out_shape=jax.ShapeDtypeStruct((NSA_HEADS, r, c), F32),
        compiler_params=_cparams(("parallel",)),
        name=name,
    )(rel_bias, jnp.asarray(bucket))


def _gelu_tanh(x):
    return 0.5 * x * (1.0 + jnp.tanh(math.sqrt(2.0 / math.pi) * (x + 0.044715 * x * x * x)))


def _compress_rows(load_rows, n_blocks, phi_ref, w1_ref, w2_ref, flat_s, hi_s, outs):
    hd = NSA_HEAD_DIM
    gc = flat_s.shape[1]
    gb = gc - V7X_SUBLANES
    nh1 = w1_ref.shape[2] // 2
    turn = 0
    for c in range(2):
        phi_c = phi_ref[:, c * hd:(c + 1) * hd]
        phi_flat = jnp.concatenate(
            [jnp.concatenate([phi_c[half * CMP_STRIDE + i:half * CMP_STRIDE + i + 1, :]
                              for i in range(CMP_STRIDE)], axis=1) for half in range(2)]
            + [jnp.zeros((V7X_SUBLANES - 2, CMP_STRIDE * hd), F32)], axis=0)
        pw = _dot(phi_flat.astype(BF16), w1_ref[c])
        const = pw[0:1, 0:nh1] + pw[1:2, nh1:2 * nh1]
        for g0 in range(0, n_blocks, gb):
            slot, turn = turn % 2, turn + 1
            for i in range(CMP_STRIDE):
                x = load_rows(c, g0 * CMP_STRIDE + i, gc, CMP_STRIDE)
                flat_s[slot, :, i * hd:(i + 1) * hd] = x.astype(BF16)
            ab = _dot(flat_s[slot], w1_ref[c])
            hi_s[slot] = ab[:, nh1:2 * nh1]
            hid = _gelu_tanh(ab[0:gb, 0:nh1] + hi_s[slot, 1:gb + 1, :] + const)
            outs[c][0, g0:g0 + gb, :] = _dot(hid.astype(BF16), w2_ref[c].astype(BF16))


def _cmp_prompt_body(x_ref, phi_ref, w1_ref, w2_ref, ck_ref, cv_ref, xpad_s, flat_s, hi_s, *, n_blocks):
    seq = x_ref.shape[0]
    hd = NSA_HEAD_DIM
    for c in range(2):
        xpad_s[c, 0:seq, :] = x_ref[:, c * hd:(c + 1) * hd]
        xpad_s[c, seq:, :] = jnp.zeros((xpad_s.shape[1] - seq, hd), F32)
    load = lambda c, first, count, stride: xpad_s[c, pl.ds(first, count, stride=stride), :]
    _compress_rows(load, n_blocks, phi_ref, w1_ref, w2_ref, flat_s, hi_s, (ck_ref, cv_ref))


def _cmp_prompt(z, phi, w1, w2, *, batch, seq):
    hd = NSA_HEAD_DIM
    n_blocks = seq // CMP_STRIDE
    n_chunks = n_blocks + V7X_SUBLANES
    pad_rows = n_chunks * CMP_STRIDE
    out = jax.ShapeDtypeStruct((batch, n_blocks, hd), F32)
    return pl.pallas_call(
        functools.partial(_cmp_prompt_body, n_blocks=n_blocks),
        grid=(batch,),
        in_specs=[
            pl.BlockSpec((seq, 2 * hd), lambda b: (b, Z_KV // (2 * hd))),
            pl.BlockSpec(phi.shape, lambda b: (0, 0)),
            pl.BlockSpec(w1.shape, lambda b: (0, 0, 0)),
            pl.BlockSpec(w2.shape, lambda b: (0, 0, 0)),
        ],
        out_specs=[pl.BlockSpec((1, n_blocks, hd), lambda b: (b, 0, 0))] * 2,
        out_shape=[out, out],
        scratch_shapes=[pltpu.VMEM((2, pad_rows, hd), F32),
                        pltpu.VMEM((2, n_chunks, CMP_STRIDE * hd), BF16),
                        pltpu.VMEM((2, n_chunks, w1.shape[2] // 2), F32)],
        compiler_params=_cparams(("parallel",)),
        name="nsa_cmp_prompt",
    )(z, phi, w1, w2)


def _cmp_sample_body(pt_ref, new_ref, cache_ref, phi_ref, w1_ref, w2_ref, ck_ref, cv_ref,
                     buf, sem, flat_s, hi_s, *, n_seq, n_pages, page, n_blocks):
    s_id = pl.program_id(0)
    slot = s_id % 2
    past = n_pages * page
    ds = new_ref.shape[0]

    hd = NSA_HEAD_DIM

    @pl.when(s_id == 0)
    def _first():
        _page_gather(pt_ref, cache_ref, _row_pages(buf, 0, 2 * page), sem.at[0], 0, 0, n_pages, True)

    @pl.when(s_id + 1 < n_seq)
    def _prefetch():
        nxt = (s_id + 1) % 2
        _page_gather(pt_ref, cache_ref, _row_pages(buf, nxt, 2 * page), sem.at[nxt], s_id + 1, 0,
                     n_pages, True)

    _store_kv_rows(buf, slot, past, new_ref[:, 0:2 * hd])
    buf[slot, 2 * (past + ds):, :] = jnp.zeros((buf.shape[1] - 2 * (past + ds), hd), F32)
    _page_gather(pt_ref, cache_ref, _row_pages(buf, slot, 2 * page), sem.at[slot], s_id, 0, n_pages, False)
    load = lambda c, first, count, stride: buf[slot, pl.ds(2 * first + c, count, stride=2 * stride), :]
    _compress_rows(load, n_blocks, phi_ref, w1_ref, w2_ref, flat_s, hi_s, (ck_ref, cv_ref))


def _store_kv_rows(buf, slot, first_token, kv):
    hd = NSA_HEAD_DIM
    for c in range(2):
        buf[slot, pl.ds(2 * first_token + c, kv.shape[0], stride=2), :] = kv[:, c * hd:(c + 1) * hd]


def _cmp_sample(page_table, z, cache, phi, w1, w2, *, row0, ds):
    n_seq, n_pages = page_table.shape
    page = cache.shape[1] // 2
    hd = NSA_HEAD_DIM
    past = n_pages * page
    n_chunk = -(-(past + ds) // CMP_STRIDE)
    n_blocks = n_chunk - CMP_BLOCK // CMP_STRIDE + 1
    gb = min(n_blocks, 512)
    assert n_blocks % gb == 0 and gb % V7X_SUBLANES == 0 and row0 % ds == 0
    gc = gb + V7X_SUBLANES
    buf_rows = (n_blocks + V7X_SUBLANES) * CMP_STRIDE
    r0 = row0 // ds
    out = jax.ShapeDtypeStruct((n_seq, n_blocks, hd), F32)
    grid_spec = pltpu.PrefetchScalarGridSpec(
        num_scalar_prefetch=1,
        grid=(n_seq,),
        in_specs=[
            pl.BlockSpec((ds, 2 * hd), lambda s, pt: (r0 + s, Z_KV // (2 * hd))),
            pl.BlockSpec(memory_space=pl.ANY),
            pl.BlockSpec(phi.shape, lambda s, pt: (0, 0)),
            pl.BlockSpec(w1.shape, lambda s, pt: (0, 0, 0)),
            pl.BlockSpec(w2.shape, lambda s, pt: (0, 0, 0)),
        ],
        out_specs=[pl.BlockSpec((1, n_blocks, hd), lambda s, pt: (s, 0, 0))] * 2,
        scratch_shapes=[
            pltpu.VMEM((2, 2 * buf_rows, hd), F32),
            pltpu.SemaphoreType.DMA((2,)),
            pltpu.VMEM((2, gc, CMP_STRIDE * hd), BF16),
            pltpu.VMEM((2, gc, w1.shape[2] // 2), F32),
        ],
    )
    body = functools.partial(_cmp_sample_body, n_seq=n_seq, n_pages=n_pages, page=page,
                             n_blocks=n_blocks)
    return pl.pallas_call(
        body,
        grid_spec=grid_spec,
        out_shape=[out, out],
        compiler_params=_cparams(("arbitrary",)),
        name="nsa_cmp_sample",
    )(page_table, z, cache, phi, w1, w2)


def _softmax_rows(s, valid):
    s = jnp.where(valid, s, NEG_INF)
    m = jnp.max(s, axis=-1, keepdims=True)
    p = jnp.where(valid, jnp.exp(s - m), 0.0)
    l = jnp.sum(p, axis=-1, keepdims=True)
    return p / jnp.where(l > 0.0, l, 1.0)


def _flash_update(s, valid, v, m_s, l_s, acc_s):
    nh, nq, tk = s.shape
    s = jnp.where(valid, s, NEG_INF)
    m_prev = m_s[...]
    m_new = jnp.maximum(m_prev, jnp.max(s, axis=-1, keepdims=True))
    alpha = jnp.exp(m_prev - m_new)
    p = jnp.where(valid, jnp.exp(s - m_new), 0.0)
    l_s[...] = alpha * l_s[...] + jnp.sum(p, axis=-1, keepdims=True)
    pv = _dot(p.astype(BF16).reshape(nh * nq, tk), v)
    acc_s[...] = alpha * acc_s[...] + pv.reshape(nh, nq, v.shape[1])
    m_s[...] = m_new


def _flash_reset(m_s, l_s, acc_s):
    m_s[...] = jnp.full(m_s.shape, NEG_INF, F32)
    l_s[...] = jnp.zeros(l_s.shape, F32)
    acc_s[...] = jnp.zeros(acc_s.shape, F32)


def _nsa_prompt_body(q_ref, g_ref, ck_ref, cv_ref, slc_ref, win_ref, bank_ref, bankc_ref,
                     cover_ref, expand_ref, o_ref,
                     qs_s, sel_s, oc_s, m_s, l_s, acc_s, os_s, *, tk, n_slc_blocks):
    qb = pl.program_id(1)
    nq = q_ref.shape[0]
    hd = NSA_HEAD_DIM
    nh = NSA_HEADS
    rows = nh * nq
    start = qb * nq
    scale = hd ** -0.5
    qs_s[...] = jnp.concatenate(
        [(q_ref[:, h * hd:(h + 1) * hd] * scale).astype(BF16) for h in range(nh)], axis=0)

    nc = ck_ref.shape[1]
    s_c = _dot_nt(qs_s[...], ck_ref[0].astype(BF16)) + bankc_ref[...].reshape(rows, nc)
    i_row = lax.broadcasted_iota(I32, (rows, nc), 0) & (nq - 1)
    n_col = lax.broadcasted_iota(I32, (rows, nc), 1)
    valid_c = (n_col * CMP_STRIDE + (CMP_BLOCK - 1)) <= (start + i_row)
    p_c = _softmax_rows(s_c, valid_c)
    oc_s[...] = _dot(p_c.astype(BF16), cv_ref[0].astype(BF16))

    ph = p_c[0:nq, :]
    for h in range(1, nh):
        ph = ph + p_c[h * nq:(h + 1) * nq, :]
    hi, lo = _split_bf16(ph)
    imp = _dot(hi, cover_ref[...]) + _dot(lo, cover_ref[...])
    ns_pad = imp.shape[1]
    blk = lax.broadcasted_iota(I32, (nq, ns_pad), 1)
    qpos = start + lax.broadcasted_iota(I32, (nq, ns_pad), 0)
    cur = lax.shift_right_logical(qpos, int(math.log2(SLC_BLOCK)))
    forced = (blk == 0) | (blk == cur) | (blk == cur - 1)
    visible = (blk * SLC_BLOCK) <= qpos
    score = jnp.where(forced, FORCED_SCORE, jnp.where(visible, imp, -1.0))
    score = jnp.where(blk < n_slc_blocks, score, PAD_SCORE)
    rank = jnp.zeros((nq, ns_pad), F32)
    for sp in range(n_slc_blocks):
        col = score[:, sp:sp + 1]
        ahead = (col > score) | ((col == score) & (blk > sp))
        rank = rank + jnp.where(ahead, 1.0, 0.0)
    sel = jnp.where((rank < float(min(N_SELECT, n_slc_blocks))) & (blk < n_slc_blocks), 1.0, 0.0)
    selx = _dot(sel.astype(BF16), expand_ref[...])
    n_kt = selx.shape[1] // tk
    for kt in range(n_kt):
        sel_s[kt] = selx[:, kt * tk:(kt + 1) * tk]

    i3 = lax.broadcasted_iota(I32, (nq, tk), 0)
    j3 = lax.broadcasted_iota(I32, (nq, tk), 1)
    n_bank = bank_ref.shape[0]
    half = tk // V7X_LANES

    def bias_tile(r0_tiles):
        parts = [bank_ref[jnp.clip(r0_tiles + u, 0, n_bank - 1)] for u in range(half)]
        return jnp.concatenate(parts, axis=1)

    _flash_reset(m_s, l_s, acc_s)
    assert nq == V7X_LANES
    org_tiles = qb - (WINDOW + Q_BLOCK) // V7X_LANES

    def slc_step(kt, carry):
        k0 = pl.multiple_of(kt * tk, tk)
        kv = slc_ref[pl.ds(k0, tk), :]
        s = _dot_nt(qs_s[...], kv[:, 0:hd].astype(BF16))
        s = s + bias_tile(kt * half - org_tiles)
        valid = (sel_s[kt] > 0.5) & ((k0 + j3) <= (start + i3))
        s3 = s.reshape(nh, nq, tk)
        _flash_update(s3, valid[None], kv[:, hd:2 * hd].astype(BF16), m_s, l_s, acc_s)
        return carry

    lax.fori_loop(0, (start + nq + tk - 1) // tk, slc_step, 0)
    os_s[...] = acc_s[...] / l_s[...]

    _flash_reset(m_s, l_s, acc_s)
    pad = WINDOW + Q_BLOCK
    for d in range(-(-(pad + nq) // tk)):
        r0 = d * tk
        kv = win_ref[0, pl.ds(start + r0, tk), :]
        kpos = start - pad + r0 + j3
        s = _dot_nt(qs_s[...], kv[:, 0:hd].astype(BF16)) + bias_tile(d * half)
        qp = start + i3
        valid = (kpos <= qp) & (kpos > qp - WINDOW) & (kpos >= 0)
        _flash_update(s.reshape(nh, nq, tk), valid[None], kv[:, hd:2 * hd].astype(BF16),
                      m_s, l_s, acc_s)
    o_w = acc_s[...] / l_s[...]

    gate = jax.nn.sigmoid(g_ref[...])
    for h in range(nh):
        o_ref[:, h * hd:(h + 1) * hd] = (
            gate[:, 3 * h:3 * h + 1] * oc_s[h * nq:(h + 1) * nq, :]
            + gate[:, 3 * h + 1:3 * h + 2] * os_s[h]
            + gate[:, 3 * h + 2:3 * h + 3] * o_w[h])


def _nsa_prompt(z, cmp_k, cmp_v, win_pad, bank, bank_c, cover, expand, *, batch, seq, tk=512):
    nqb = seq // Q_BLOCK
    hd = NSA_HEAD_DIM
    nh = NSA_HEADS
    rows = nh * Q_BLOCK
    nc = cmp_k.shape[1]
    n_slc_blocks = -(-seq // SLC_BLOCK)
    win_span = -(-(WINDOW + 2 * Q_BLOCK) // tk) * tk
    assert seq % tk == 0 and win_pad.shape[1] >= seq - Q_BLOCK + win_span
    body = functools.partial(_nsa_prompt_body, tk=tk, n_slc_blocks=n_slc_blocks)
    return pl.pallas_call(
        body,
        grid=(batch, nqb),
        in_specs=[
            pl.BlockSpec((Q_BLOCK, nh * hd), lambda b, q: (b * nqb + q, Z_Q // (nh * hd))),
            pl.BlockSpec((Q_BLOCK, V7X_LANES), lambda b, q: (b * nqb + q, Z_G // V7X_LANES)),
            pl.BlockSpec((1, nc, hd), lambda b, q: (b, 0, 0)),
            pl.BlockSpec((1, nc, hd), lambda b, q: (b, 0, 0)),
            pl.BlockSpec((seq, 2 * hd), lambda b, q: (b, Z_KV // (2 * hd) + 1)),
            pl.BlockSpec((1,) + win_pad.shape[1:], lambda b, q: (b, 0, 0)),
            pl.BlockSpec(bank.shape, lambda b, q: (0, 0, 0)),
            pl.BlockSpec((nh, Q_BLOCK, nc), lambda b, q: (0, q, 0)),
            pl.BlockSpec(cover.shape, lambda b, q: (0, 0)),
            pl.BlockSpec(expand.shape, lambda b, q: (0, 0)),
        ],
        out_specs=pl.BlockSpec((Q_BLOCK, nh * hd), lambda b, q: (b * nqb + q, 0)),
        out_shape=jax.ShapeDtypeStruct((batch * seq, nh * hd), F32),
        scratch_shapes=[
            pltpu.VMEM((rows, hd), BF16),
            pltpu.VMEM((seq // tk, Q_BLOCK, tk), F32),
            pltpu.VMEM((rows, hd), F32),
            pltpu.VMEM((nh, Q_BLOCK, 1), F32),
            pltpu.VMEM((nh, Q_BLOCK, 1), F32),
            pltpu.VMEM((nh, Q_BLOCK, hd), F32),
            pltpu.VMEM((nh, Q_BLOCK, hd), F32),
        ],
        compiler_params=_cparams(("parallel", "arbitrary")),
        name="nsa_prompt",
    )(z, z, cmp_k, cmp_v, z, win_pad, bank, bank_c, cover, expand)


def _nsa_sample_body(pt_ref, q_ref, g_ref, new_ref, ck_ref, cv_ref, win_ref, cache_ref,
                     bankc_ref, bankw_ref, banks_ref, bankt_ref, cover_ref, expand_ref,
                     o_ref,
                     buf, sem, qs_s, wkv_s, selx_s, m_s, l_s, acc_s,
                     *, n_seq, n_pages, page, chunk, n_slc_blocks):
    s_id = pl.program_id(0)
    slot = s_id % 2
    ds = q_ref.shape[0]
    hd = NSA_HEAD_DIM
    nh = NSA_HEADS
    rows = nh * ds
    past = n_pages * page
    n_chunks = past // chunk
    scale = hd ** -0.5

    @pl.when(s_id == 0)
    def _first():
        _page_gather(pt_ref, cache_ref, _row_pages(buf, 0, 2 * page), sem.at[0], 0, 0, n_pages, True)

    @pl.when(s_id + 1 < n_seq)
    def _prefetch():
        nxt = (s_id + 1) % 2
        _page_gather(pt_ref, cache_ref, _row_pages(buf, nxt, 2 * page), sem.at[nxt], s_id + 1, 0,
                     n_pages, True)

    new_kv = new_ref[...]
    tail = buf.shape[1] // 2 - past
    _store_kv_rows(buf, slot, past, new_kv[:, 2 * hd:4 * hd])
    buf[slot, 2 * (past + ds):, :] = jnp.zeros((2 * (tail - ds), hd), F32)

    def load_kv(first, count):
        return (buf[slot, pl.ds(2 * first, count, stride=2), :].astype(BF16),
                buf[slot, pl.ds(2 * first + 1, count, stride=2), :].astype(BF16))

    qs = jnp.concatenate(
        [(q_ref[:, h * hd:(h + 1) * hd] * scale).astype(BF16) for h in range(nh)], axis=0)
    qs_s[...] = qs
    t_col = lax.broadcasted_iota(I32, (rows, 1), 0) & (ds - 1)

    nc = ck_ref.shape[1]
    s_c = _dot_nt(qs, ck_ref[0].astype(BF16)) + bankc_ref[...]
    n_col = lax.broadcasted_iota(I32, (rows, nc), 1)
    valid_c = (n_col * CMP_STRIDE + (CMP_BLOCK - 1)) <= (past + t_col)
    p_c = _softmax_rows(s_c, valid_c)
    o_c = _dot(p_c.astype(BF16), cv_ref[0].astype(BF16))

    ph = p_c[0:ds, :]
    for h in range(1, nh):
        ph = ph + p_c[h * ds:(h + 1) * ds, :]
    hi, lo = _split_bf16(ph)
    imp = _dot(hi, cover_ref[...]) + _dot(lo, cover_ref[...])
    ns_pad = imp.shape[1]
    blk = lax.broadcasted_iota(I32, (ds, ns_pad), 1)
    qpos = past + lax.broadcasted_iota(I32, (ds, ns_pad), 0)
    cur = lax.shift_right_logical(qpos, int(math.log2(SLC_BLOCK)))
    forced = (blk == 0) | (blk == cur) | (blk == cur - 1)
    visible = (blk * SLC_BLOCK) <= qpos
    score = jnp.where(forced, FORCED_SCORE, jnp.where(visible, imp, -1.0))
    score = jnp.where(blk < n_slc_blocks, score, PAD_SCORE)
    score_t = jnp.concatenate([score, jnp.zeros((V7X_LANES - ds, ns_pad), F32)], axis=0).T
    sp_idx = lax.broadcasted_iota(I32, (ns_pad, ns_pad), 0)
    s_idx = lax.broadcasted_iota(I32, (ns_pad, ns_pad), 1)
    ranks = []
    for t in range(ds):
        c = score_t[:, t:t + 1]
        r = score[t:t + 1, :]
        ahead = (c > r) | ((c == r) & (sp_idx < s_idx))
        ranks.append(jnp.sum(jnp.where(ahead, 1.0, 0.0), axis=0, keepdims=True))
    rank = jnp.concatenate(ranks, axis=0)
    sel = jnp.where((rank < float(min(N_SELECT, n_slc_blocks))) & (blk < n_slc_blocks), 1.0, 0.0)

    bpc = chunk // SLC_BLOCK
    n_groups = selx_s.shape[0] // ds
    lane = lax.broadcasted_iota(I32, (ds, V7X_LANES), 1)
    pieces = []
    for gi in range(n_groups):
        a, off = (gi * bpc) // V7X_LANES, (gi * bpc) % V7X_LANES
        tile = sel[:, a * V7X_LANES:(a + 1) * V7X_LANES]
        if off:
            tile = pltpu.roll(tile, V7X_LANES - off, 1)
        pieces.append(jnp.where(lane < bpc, tile, 0.0))
    selx_s[...] = _dot(jnp.concatenate(pieces, axis=0).astype(BF16), expand_ref[...])

    _page_gather(pt_ref, cache_ref, _row_pages(buf, slot, 2 * page), sem.at[slot], s_id, 0, n_pages, False)
    _flash_reset(m_s, l_s, acc_s)
    far_bias = banks_ref[:, 0:1].reshape(nh, ds, 1)

    def slc_chunk(c, bias):
        k, v = load_kv(pl.multiple_of(c * chunk, chunk), chunk)
        s = _dot_nt(qs_s[...], k).reshape(nh, ds, chunk) + bias
        valid = selx_s[pl.ds(pl.multiple_of(c * ds, ds), ds), :] > 0.5
        _flash_update(s, valid[None], v, m_s, l_s, acc_s)

    def slc_step(c, carry):
        slc_chunk(c, far_bias)
        return carry

    lax.fori_loop(0, n_chunks - 1, slc_step, 0, unroll=True)
    slc_chunk(n_chunks - 1, banks_ref[...].reshape(nh, ds, chunk))

    k_t, v_t = load_kv(past, tail)
    s_t = _dot_nt(qs, k_t) + bankt_ref[...]
    j_t = lax.broadcasted_iota(I32, (ds, tail), 1)
    t_t = lax.broadcasted_iota(I32, (ds, tail), 0)
    valid_t = (selx_s[n_chunks * ds:(n_chunks + 1) * ds, 0:tail] > 0.5) & (j_t <= t_t)
    _flash_update(s_t.reshape(nh, ds, tail), valid_t[None], v_t, m_s, l_s, acc_s)
    o_s = acc_s[...] / l_s[...]

    wb = win_ref.shape[1] // 2
    nw = wkv_s.shape[1]
    for c in range(2):
        wkv_s[c, 0:wb, :] = win_ref[0, pl.ds(c, wb, stride=2), :]
        wkv_s[c, wb:wb + ds, :] = new_kv[:, (4 + c) * hd:(5 + c) * hd]
        wkv_s[c, wb + ds:, :] = jnp.zeros((nw - wb - ds, hd), F32)
    s_w = _dot_nt(qs, wkv_s[0].astype(BF16)) + bankw_ref[...]
    w_col = lax.broadcasted_iota(I32, (rows, nw), 1)
    valid_w = ((w_col - wb) <= t_col) & ((w_col - wb) > (t_col - WINDOW)) & (w_col < wb + ds)
    p_w = _softmax_rows(s_w, valid_w)
    o_w = _dot(p_w.astype(BF16), wkv_s[1].astype(BF16))

    gate = jax.nn.sigmoid(g_ref[...])
    for h in range(nh):
        o_ref[:, h * hd:(h + 1) * hd] = (
            gate[:, 3 * h:3 * h + 1] * o_c[h * ds:(h + 1) * ds, :]
            + gate[:, 3 * h + 1:3 * h + 2] * o_s[h]
            + gate[:, 3 * h + 2:3 * h + 3] * o_w[h * ds:(h + 1) * ds, :])


def _nsa_sample(page_table, z, cmp_k, cmp_v, win_buf, cache, bank_c, bank_w, bank_s, bank_t,
                cover, expand, *, row0, ds, chunk=2048):
    n_seq, n_pages = page_table.shape
    page = cache.shape[1] // 2
    hd = NSA_HEAD_DIM
    nh = NSA_HEADS
    rows = nh * ds
    past = n_pages * page
    assert past % chunk == 0 and row0 % ds == 0 and chunk // SLC_BLOCK <= V7X_LANES
    n_slc_blocks = -(-(past + ds) // SLC_BLOCK)
    tail = V7X_LANES
    n_groups = past // chunk + 1
    nc = cmp_k.shape[1]
    wb2 = win_buf.shape[1]
    nw = bank_w.shape[1]
    r0 = row0 // ds
    body = functools.partial(_nsa_sample_body, n_seq=n_seq, n_pages=n_pages, page=page,
                             chunk=chunk, n_slc_blocks=n_slc_blocks)
    const2 = lambda s, pt: (0, 0)
    grid_spec = pltpu.PrefetchScalarGridSpec(
        num_scalar_prefetch=1,
        grid=(n_seq,),
        in_specs=[
            pl.BlockSpec((ds, nh * hd), lambda s, pt: (r0 + s, Z_Q // (nh * hd))),
            pl.BlockSpec((ds, V7X_LANES), lambda s, pt: (r0 + s, Z_G // V7X_LANES)),
            pl.BlockSpec((ds, 6 * hd), lambda s, pt: (r0 + s, Z_KV // (6 * hd))),
            pl.BlockSpec((1, nc, hd), lambda s, pt: (s, 0, 0)),
            pl.BlockSpec((1, nc, hd), lambda s, pt: (s, 0, 0)),
            pl.BlockSpec((1, wb2, hd), lambda s, pt: (s, 0, 0)),
            pl.BlockSpec(memory_space=pl.ANY),
            pl.BlockSpec(bank_c.shape, const2),
            pl.BlockSpec(bank_w.shape, const2),
            pl.BlockSpec(bank_s.shape, const2),
            pl.BlockSpec(bank_t.shape, const2),
            pl.BlockSpec(cover.shape, const2),
            pl.BlockSpec(expand.shape, const2),
        ],
        out_specs=pl.BlockSpec((ds, nh * hd), lambda s, pt: (s, 0)),
        scratch_shapes=[
            pltpu.VMEM((2, 2 * (past + tail), hd), F32),
            pltpu.SemaphoreType.DMA((2,)),
            pltpu.VMEM((rows, hd), BF16),
            pltpu.VMEM((2, nw, hd), F32),
            pltpu.VMEM((n_groups * ds, chunk), F32),
            pltpu.VMEM((nh, ds, 1), F32),
            pltpu.VMEM((nh, ds, 1), F32),
            pltpu.VMEM((nh, ds, hd), F32),
        ],
    )
    return pl.pallas_call(
        body,
        grid_spec=grid_spec,
        out_shape=jax.ShapeDtypeStruct((n_seq * ds, nh * hd), F32),
        compiler_params=_cparams(("arbitrary",)),
        name="nsa_sample",
    )(page_table, z, z, z, cmp_k, cmp_v, win_buf, cache, bank_c, bank_w, bank_s, bank_t,
      cover, expand)


def _conv_silu(g, g1, g2, u, cw_ref, cb_ref, col0, n_valid):
    cw = cw_ref[...]
    conv = cb_ref[...] + cw[0:1] * g2 + cw[1:2] * g1 + cw[2:3] * g
    act = jax.nn.silu(conv) * u
    col = col0 + lax.broadcasted_iota(I32, act.shape, act.ndim - 1)
    return jnp.where(col < n_valid, act, 0.0)


def _ffn_up_prompt_body(a_ref, wg_ref, wu_ref, cw_ref, cb_ref, act_ref, tail_ref,
                        wg_s, wu_s, carry_s, *, tiles_per_seq, n_valid):
    n = pl.program_id(0)
    m = pl.program_id(1)
    tm, tn = act_ref.shape

    @pl.when(m == 0)
    def _cast():
        wg_s[...] = wg_ref[...].astype(BF16)
        wu_s[...] = wu_ref[...].astype(BF16)

    @pl.when(m % tiles_per_seq == 0)
    def _seq_start():
        carry_s[...] = jnp.zeros_like(carry_s)

    a = a_ref[...]
    g = _dot(a, wg_s[...])
    u = _dot(a, wu_s[...])
    prev = carry_s[...]
    row = lax.broadcasted_iota(I32, (tm, tn), 0)
    last = V7X_SUBLANES - 1
    g1 = jnp.where(row == 0, prev[last:last + 1], pltpu.roll(g, 1, 0))
    g2 = jnp.where(row == 0, prev[last - 1:last],
                   jnp.where(row == 1, prev[last:last + 1], pltpu.roll(g, 2, 0)))
    act_ref[...] = _conv_silu(g, g1, g2, u, cw_ref, cb_ref, n * tn, n_valid).astype(act_ref.dtype)
    carry_s[...] = g[tm - V7X_SUBLANES:tm, :]
    tail_ref[0] = g[tm - V7X_SUBLANES:tm, :]


def _ffn_up_prompt(hn, w_gate, w_up, conv_w, conv_b, *, rows, seq, tm=1024, tn=512):
    d, dff = w_gate.shape
    n_blocks = pl.cdiv(dff, tn)
    assert rows % tm == 0 and seq % tm == 0
    body = functools.partial(_ffn_up_prompt_body, tiles_per_seq=seq // tm, n_valid=dff)
    once = pl.Buffered(1)
    return pl.pallas_call(
        body,
        grid=(n_blocks, rows // tm),
        in_specs=[
            pl.BlockSpec((tm, d), lambda n, m: (m, 0)),
            pl.BlockSpec((d, tn), lambda n, m: (0, n), pipeline_mode=once),
            pl.BlockSpec((d, tn), lambda n, m: (0, n), pipeline_mode=once),
            pl.BlockSpec((conv_w.shape[0], tn), lambda n, m: (0, n)),
            pl.BlockSpec((1, tn), lambda n, m: (0, n)),
        ],
        out_specs=[
            pl.BlockSpec((tm, tn), lambda n, m: (m, n)),
            pl.BlockSpec((1, V7X_SUBLANES, tn), lambda n, m: (m, 0, n)),
        ],
        out_shape=[
            jax.ShapeDtypeStruct((rows, dff), BF16),
            jax.ShapeDtypeStruct((rows // tm, V7X_SUBLANES, dff), F32),
        ],
        scratch_shapes=[pltpu.VMEM((d, tn), BF16), pltpu.VMEM((d, tn), BF16),
                        pltpu.VMEM((V7X_SUBLANES, tn), F32)],
        compiler_params=_cparams(("arbitrary", "arbitrary")),
        name="ffn_up_prompt",
    )(hn, w_gate, w_up, conv_w, conv_b.reshape(1, dff))


def _ffn_up_sample_body(a_ref, wg_ref, wu_ref, cw_ref, cb_ref, st_ref, act_ref, st_out_ref,
                        *, ds, n_valid):
    n = pl.program_id(0)
    tm, tn = act_ref.shape
    nseq = tm // ds
    a = a_ref[...]
    g = _dot(a, wg_ref[...].astype(BF16)).reshape(nseq, ds, tn)
    u = _dot(a, wu_ref[...].astype(BF16)).reshape(nseq, ds, tn)
    st = st_ref[...]
    s0, s1 = st[:, 0:1, :], st[:, 1:2, :]
    t = lax.broadcasted_iota(I32, (nseq, ds, tn), 1)
    g1 = jnp.where(t == 0, s1, pltpu.roll(g, 1, 1))
    g2 = jnp.where(t == 0, s0, jnp.where(t == 1, s1, pltpu.roll(g, 2, 1)))
    cw = cw_ref[...]
    conv = cb_ref[...] + cw[0:1] * g2 + cw[1:2] * g1 + cw[2:3] * g
    act = jax.nn.silu(conv) * u
    col = n * tn + lax.broadcasted_iota(I32, act.shape, 2)
    act = jnp.where(col < n_valid, act, 0.0)
    act_ref[...] = act.reshape(tm, tn).astype(act_ref.dtype)
    st_out_ref[...] = g[:, ds - 2:ds, :]


def _ffn_up_sample(hn, w_gate, w_up, conv_w, conv_b, state, *, row0, rows, ds, tn=256):
    d, dff = w_gate.shape
    n_blocks = pl.cdiv(dff, tn)
    assert row0 % rows == 0 and rows % ds == 0 and ds == V7X_SUBLANES
    nseq = rows // ds
    r0 = row0 // rows
    body = functools.partial(_ffn_up_sample_body, ds=ds, n_valid=dff)
    return pl.pallas_call(
        body,
        grid=(n_blocks,),
        in_specs=[
            pl.BlockSpec((rows, d), lambda n: (r0, 0)),
            pl.BlockSpec((d, tn), lambda n: (0, n)),
            pl.BlockSpec((d, tn), lambda n: (0, n)),
            pl.BlockSpec((conv_w.shape[0], tn), lambda n: (0, n)),
            pl.BlockSpec((1, tn), lambda n: (0, n)),
            pl.BlockSpec((nseq, state.shape[1], tn), lambda n: (0, 0, n)),
        ],
        out_specs=[
            pl.BlockSpec((rows, tn), lambda n: (0, n)),
            pl.BlockSpec((nseq, state.shape[1], tn), lambda n: (0, 0, n)),
        ],
        out_shape=[
            jax.ShapeDtypeStruct((rows, dff), BF16),
            jax.ShapeDtypeStruct(state.shape, F32),
        ],
        compiler_params=_cparams(("parallel",)),
        name="ffn_up_sample",
    )(hn, w_gate, w_up, conv_w, conv_b.reshape(1, dff), state)


def _layout_w_in(w_in):
    d = w_in.shape[0]
    nq = NSA_HEADS * NSA_HEAD_DIM
    nkv = 6 * NSA_HEAD_DIM
    ng = 3 * NSA_HEADS
    o_kv, o_g = nq, nq + nkv
    o_qa = o_g + ng
    o_kva = o_qa + MLA_Q_LORA
    o_m = o_kva + MLA_KV_LORA + MLA_ROPE
    d_model = (w_in.shape[1] - o_m) // 2

    def zeros(n):
        return jnp.zeros((d, n), w_in.dtype)

    parts = [
        w_in[:, 0:nq],
        w_in[:, o_g:o_g + ng], zeros(Z_QA - Z_G - ng),
        w_in[:, o_qa:o_qa + MLA_Q_LORA],
        w_in[:, o_kv:o_kv + nkv],
        w_in[:, o_kva + MLA_KV_LORA:o_kva + MLA_KV_LORA + MLA_ROPE],
        zeros(Z_CKV - Z_KPE - MLA_ROPE),
        w_in[:, o_kva:o_kva + MLA_KV_LORA],
        w_in[:, o_m:o_m + 2 * d_model],
    ]
    out = jnp.concatenate(parts, axis=1).astype(BF16)
    assert out.shape[1] == Z_WIDTH and Z_MB - Z_MA == d_model
    return out


def _rope_tables(pos):
    half = MLA_ROPE // 2
    inv = ROPE_THETA ** (-jnp.arange(half, dtype=F32) / half)
    ang = pos.astype(F32)[:, None] * inv
    cos, sin = jnp.cos(ang), jnp.sin(ang)
    pad = jnp.zeros((pos.shape[0], V7X_LANES - MLA_ROPE), F32)
    return (jnp.concatenate([cos, cos, pad], axis=1), jnp.concatenate([-sin, sin, pad], axis=1))


def kernel(x_prompt, x_sample, cache_mla, cache_nsa_cmp_kv, cache_nsa_slc_kv, cache_nsa_win_kv,
           state_ffn_conv, page_table, rel_bias, norm_attn, w_in, nsa_phi_pos, nsa_phi_w1, nsa_phi_w2,
           mla_q_norm, mla_w_uq, mla_kv_norm, mla_w_ukv, w_br_nsa, w_br_mla, w_out, norm_ffn,
           w_ffn_gate, w_ffn_up, ffn_conv_w, ffn_conv_b, w_ffn_down, norm_final):
    depth = norm_attn.shape[0]
    assert depth == 1, "the layer loop is written for the single-layer configuration"
    lyr = 0
    b, seq, d = x_prompt.shape
    db, ds, _ = x_sample.shape
    mp, msm = b * seq, db * ds
    m = mp + msm
    n_pool, page = cache_mla.shape[1], cache_mla.shape[2]
    n_pages = page_table.shape[1]
    past = n_pages * page
    hd = NSA_HEAD_DIM
    nh = NSA_HEADS
    assert seq % Q_BLOCK == 0 and ds == V7X_SUBLANES and past >= WINDOW
    assert cache_nsa_win_kv.shape[2] == WINDOW and seq >= WINDOW

    x_all = jnp.concatenate([x_prompt.reshape(mp, d), x_sample.reshape(msm, d)], axis=0)
    xn = _rmsnorm(x_all, norm_attn[lyr], BF16, name="norm_attn")
    z = _matmul(xn, _layout_w_in(w_in[lyr]), tm=1024, tn=1280, tk=2048, out_dtype=F32, name="in_proj")

    pos_all = jnp.concatenate([jnp.tile(jnp.arange(seq, dtype=I32), b),
                               jnp.tile(past + jnp.arange(ds, dtype=I32), db)])
    cs, sn = _rope_tables(pos_all)
    w_uq = mla_w_uq[lyr].reshape(MLA_Q_LORA, MLA_HEADS, MLA_NOPE + MLA_ROPE)
    w_uq_p = jnp.concatenate([
        w_uq[:, :, :MLA_NOPE].reshape(MLA_Q_LORA, MLA_HEADS * MLA_NOPE),
        jnp.pad(w_uq[:, :, MLA_NOPE:], ((0, 0), (0, 0), (0, V7X_LANES - MLA_ROPE))).reshape(
            MLA_Q_LORA, MLA_HEADS * V7X_LANES)], axis=1).astype(BF16)
    w_ukv = mla_w_ukv[lyr]
    wk = jnp.transpose(w_ukv[:, :, :MLA_NOPE], (1, 2, 0)).astype(BF16)
    w_uv = jnp.transpose(w_ukv[:, :, MLA_NOPE:], (1, 0, 2)).astype(BF16)
    q_lat, q_pe, lat = _mla_project(z, cs, sn, mla_q_norm[lyr], w_uq_p, mla_kv_norm[lyr], wk)

    pt = page_table + lyr * n_pool
    lat_w = MLA_KV_LORA + MLA_ROPE
    o_lat_p = _mla_prompt(q_lat, q_pe, lat, batch=b, seq=seq)
    cache_mla_t = jnp.swapaxes(cache_mla, 2, 3).reshape(depth * n_pool, lat_w, page)
    o_lat_s = _mla_sample(pt, q_lat, q_pe, lat, cache_mla_t, row0=mp, ds=ds)
    o_mla = jnp.concatenate([_mla_uv(o_lat_p, w_uv, name="mla_uv_prompt"),
                             _mla_uv(o_lat_s, w_uv, name="mla_uv_sample")], axis=0)

    pad = WINDOW + Q_BLOCK
    i_q = np.arange(Q_BLOCK)[:, None]
    bank = _bias_bank(rel_bias, _rel_bucket_np(i_q + pad - np.arange(8 * V7X_LANES)[None, :]), "bias_bank_prompt")
    bank = jnp.transpose(bank.reshape(nh, Q_BLOCK, 8, V7X_LANES), (2, 0, 1, 3)).reshape(
        8, nh * Q_BLOCK, V7X_LANES)
    nc_p = seq // CMP_STRIDE
    qpos_p = np.arange(seq)[:, None]
    bank_cp = _bias_bank(rel_bias, _rel_bucket_np(
        qpos_p - (np.arange(nc_p)[None, :] * CMP_STRIDE + CMP_BLOCK - 1)), "bias_bank_prompt_cmp")
    ns_p = -(-seq // SLC_BLOCK)

    def cover_matrix(n_cmp, n_cmp_pad, n_slc, n_slc_pad):
        c_start = np.arange(n_cmp_pad)[:, None] * CMP_STRIDE
        s_start = np.arange(n_slc_pad)[None, :] * SLC_BLOCK
        cov = (c_start < s_start + SLC_BLOCK) & (c_start + CMP_BLOCK > s_start)
        cov &= (np.arange(n_cmp_pad)[:, None] < n_cmp) & (np.arange(n_slc_pad)[None, :] < n_slc)
        return jnp.asarray(cov.astype(np.float32), dtype=BF16)

    cover_p = cover_matrix(seq // CMP_STRIDE - 1, nc_p, ns_p, V7X_LANES)
    expand_p = jnp.asarray(
        (np.arange(seq)[None, :] // SLC_BLOCK == np.arange(V7X_LANES)[:, None]).astype(np.float32), dtype=BF16)
    phi = nsa_phi_pos[lyr].reshape(CMP_BLOCK, 2 * hd)
    half = CMP_STRIDE * hd
    w1 = jnp.concatenate([nsa_phi_w1[lyr][:, :half], nsa_phi_w1[lyr][:, half:]], axis=2).astype(BF16)
    w2 = nsa_phi_w2[lyr]
    cmp_k, cmp_v = _cmp_prompt(z, phi, w1, w2, batch=b, seq=seq)
    win_new = z[:, Z_KV + 4 * hd:Z_KV + 6 * hd]
    tk_nsa = 512
    win_pad = jnp.pad(win_new[:mp].reshape(b, seq, 2 * hd),
                      ((0, 0), (pad, -(pad + Q_BLOCK) % tk_nsa), (0, 0)))
    o_nsa_p = _nsa_prompt(z, cmp_k, cmp_v, win_pad, bank, bank_cp, cover_p, expand_p, batch=b, seq=seq,
                          tk=tk_nsa)

    cache_cmp = cache_nsa_cmp_kv.reshape(depth * n_pool, 2 * page, hd)
    cache_slc = cache_nsa_slc_kv.reshape(depth * n_pool, 2 * page, hd)
    cmp_ks, cmp_vs = _cmp_sample(pt, z, cache_cmp, phi, w1, w2, row0=mp, ds=ds)
    nc_s = cmp_ks.shape[1]
    chunk = 2048
    t_q = np.arange(ds)[:, None]
    wb = cache_nsa_win_kv.shape[2]
    nw = -(-(wb + ds) // V7X_LANES) * V7X_LANES

    def sample_bank(dist, name):
        return _bias_bank(rel_bias, _rel_bucket_np(dist), name).reshape(nh * ds, dist.shape[1])

    bank_cs = sample_bank(past + t_q - (np.arange(nc_s)[None, :] * CMP_STRIDE + CMP_BLOCK - 1), "bias_bank_sample_cmp")
    bank_ws = sample_bank(t_q + wb - np.arange(nw)[None, :], "bias_bank_sample_win")
    bank_ss = sample_bank(past + t_q - (past - chunk + np.arange(chunk)[None, :]), "bias_bank_sample_slc")
    bank_ts = sample_bank(t_q - np.arange(V7X_LANES)[None, :], "bias_bank_sample_new")
    ns_s = -(-(past + ds) // SLC_BLOCK)
    ns_s_pad = -(-ns_s // V7X_LANES) * V7X_LANES
    n_cmp_s = -(-(past + ds) // CMP_STRIDE) - CMP_BLOCK // CMP_STRIDE + 1
    cover_s = cover_matrix(n_cmp_s, nc_s, ns_s, ns_s_pad)
    expand_s = jnp.asarray(
        (np.arange(chunk)[None, :] // SLC_BLOCK == np.arange(V7X_LANES)[:, None]).astype(np.float32), dtype=BF16)
    o_nsa_s = _nsa_sample(pt, z, cmp_ks, cmp_vs, cache_nsa_win_kv[lyr].reshape(db, 2 * wb, hd), cache_slc,
                          bank_cs, bank_ws, bank_ss, bank_ts, cover_s, expand_s,
                          row0=mp, ds=ds, chunk=chunk)
    o_nsa = jnp.concatenate([o_nsa_p, o_nsa_s], axis=0)

    t1 = _matmul(o_nsa, w_br_nsa[lyr], tm=1024, tn=512, tk=2048, out_dtype=F32, extras=[(z, Z_MA)],
                 epilogue=lambda acc, zm: jax.nn.sigmoid(zm) * acc, name="merge_nsa")
    y = _matmul(o_mla, w_br_mla[lyr], tm=1024, tn=512, tk=2048, out_dtype=BF16,
                extras=[(z, Z_MB), (t1, 0)],
                epilogue=lambda acc, zm, t: t + jax.nn.sigmoid(zm) * acc, name="merge_mla")
    h1 = _matmul(y, w_out[lyr], tm=1024, tn=1024, tk=2048, out_dtype=F32, extras=[(x_all, 0)],
                 epilogue=lambda acc, x: x + acc, name="out_proj")

    hn = _rmsnorm(h1, norm_ffn[lyr], BF16, name="norm_ffn")
    dff = w_ffn_gate.shape[2]
    assert ffn_conv_w.shape[1] == 3 and state_ffn_conv.shape[2] == 2
    tm_f = 1024
    act_p, tails = _ffn_up_prompt(hn, w_ffn_gate[lyr], w_ffn_up[lyr], ffn_conv_w[lyr], ffn_conv_b[lyr],
                                  rows=mp, seq=seq, tm=tm_f)
    act_s, conv_s = _ffn_up_sample(hn, w_ffn_gate[lyr], w_ffn_up[lyr], ffn_conv_w[lyr], ffn_conv_b[lyr],
                                   state_ffn_conv[lyr], row0=mp, rows=msm, ds=ds)
    add = lambda acc, h: h + acc
    h2_p = _matmul(act_p, w_ffn_down[lyr], tm=1024, tn=1024, tk=1024, out_dtype=F32, extras=[(h1, 0)],
                   epilogue=add, name="ffn_down_prompt")
    h2_s = _matmul(act_s, w_ffn_down[lyr], tm=1024, tn=1024, tk=1024, out_dtype=F32, extras=[(h1, 0, mp)],
                   epilogue=add, name="ffn_down_sample")
    y_p = _rmsnorm(h2_p, norm_final, F32, name="norm_final_prompt").reshape(b, seq, d)
    y_s = _rmsnorm(h2_s, norm_final, F32, name="norm_final_sample").reshape(db, ds, d)

    kv_tail = (2, 1, hd)
    kv_new = z[:, Z_KV:Z_KV + 6 * hd]
    kv_p = kv_new[:mp].reshape(b, seq, 3, *kv_tail)
    kv_s = kv_new[mp:].reshape(db, ds, 3, *kv_tail)
    win_all = jnp.concatenate([cache_nsa_win_kv[lyr], kv_s[:, :, 2]], axis=1)
    tps = seq // tm_f
    conv_p = tails.reshape(b, tps, V7X_SUBLANES, dff)[:, tps - 1, V7X_SUBLANES - 2:, :]
    return (y_p, y_s,
            lat[:mp].reshape(1, b, seq, lat_w), lat[mp:].reshape(1, db, ds, lat_w),
            kv_p[None, :, :, 0], kv_s[None, :, :, 0],
            kv_p[None, :, :, 1], kv_s[None, :, :, 1],
            kv_p[None, :, seq - min(WINDOW, seq):, 2],
            win_all[None, :, wb + ds - min(WINDOW, wb + ds):],
            conv_p[None], conv_s[None])
```

```python
import functools
import math

import numpy as np
import jax
import jax.numpy as jnp
from jax import lax
from jax.experimental import pallas as pl
from jax.experimental.pallas import tpu as pltpu

F32, BF16, I32 = jnp.float32, jnp.bfloat16, jnp.int32

V7X_LANES = 128
V7X_SUBLANES = 8
V7X_VMEM_BYTES = 64 * 1024 * 1024
VMEM_LIMIT = V7X_VMEM_BYTES - 8 * 1024 * 1024

NSA_HEADS = 16
NSA_HEAD_DIM = 128
CMP_BLOCK = 32
CMP_STRIDE = 16
SLC_BLOCK = 64
N_SELECT = 16
WINDOW = 512
MLA_HEADS = 16
MLA_Q_LORA = 768
MLA_KV_LORA = 512
MLA_NOPE = 128
MLA_ROPE = 64
MLA_V = 128
MLA_SCALE = (MLA_NOPE + MLA_ROPE) ** -0.5
ROPE_THETA = 10000.0
REL_BUCKETS = 32
REL_MAX_DIST = 128
RMS_EPS = 1e-6
Q_BLOCK = 128
NEG_INF = -1e30
FORCED_SCORE = 1e30
PAD_SCORE = -3e38

Z_Q = 0
Z_G = 2048
Z_QA = 2304
Z_KV = 3072
Z_KPE = 3840
Z_CKV = 4096
Z_MA = 4608
Z_MB = 8704
Z_WIDTH = 12800


def _cparams(sem, vmem=VMEM_LIMIT):
    return pltpu.CompilerParams(dimension_semantics=sem, vmem_limit_bytes=vmem)


def _dot(a, b):
    return jnp.dot(a, b, preferred_element_type=F32)


def _dot_nt(a, b):
    return lax.dot_general(a, b, (((1,), (1,)), ((), ())), preferred_element_type=F32)


def _split_bf16(x):
    hi = x.astype(BF16)
    lo = (x - hi.astype(F32)).astype(BF16)
    return hi, lo


def _rmsnorm_body(x_ref, g_ref, o_ref):
    x = x_ref[...]
    ms = jnp.mean(x * x, axis=-1, keepdims=True)
    o_ref[...] = (x * lax.rsqrt(ms + RMS_EPS) * g_ref[...]).astype(o_ref.dtype)


def _rmsnorm(x, g, out_dtype, *, row0=0, rows=None, tm=256, name="rmsnorm"):
    m_all, d = x.shape
    rows = m_all if rows is None else rows
    assert rows % tm == 0 and row0 % tm == 0
    b0 = row0 // tm
    return pl.pallas_call(
        _rmsnorm_body,
        grid=(rows // tm,),
        in_specs=[pl.BlockSpec((tm, d), lambda i: (b0 + i, 0)),
                  pl.BlockSpec((1, d), lambda i: (0, 0))],
        out_specs=pl.BlockSpec((tm, d), lambda i: (i, 0)),
        out_shape=jax.ShapeDtypeStruct((rows, d), out_dtype),
        compiler_params=_cparams(("parallel",)),
        name=name,
    )(x, g.reshape(1, d))


def _mm_body(*refs, nk, tk, k_valid, n_extra, epilogue):
    a_ref, w_ref = refs[0], refs[1]
    extra = refs[2:2 + n_extra]
    o_ref, acc_ref = refs[2 + n_extra], refs[3 + n_extra]
    k = pl.program_id(2)

    @pl.when(k == 0)
    def _init():
        acc_ref[...] = jnp.zeros_like(acc_ref)

    a = a_ref[...].astype(BF16)
    w = w_ref[...].astype(BF16)
    if k_valid % tk:
        row = lax.broadcasted_iota(I32, w.shape, 0) + k * tk
        w = jnp.where(row < k_valid, w, jnp.zeros_like(w))
        col = lax.broadcasted_iota(I32, a.shape, 1) + k * tk
        a = jnp.where(col < k_valid, a, jnp.zeros_like(a))
    acc_ref[...] += _dot(a, w)

    @pl.when(k == nk - 1)
    def _fin():
        o_ref[...] = epilogue(acc_ref[...], *(e[...] for e in extra)).astype(o_ref.dtype)


def _matmul(a, w, *, tm, tn, tk, out_dtype, a_col0=0, extras=(), epilogue=None, name):
    m = a.shape[0]
    k_valid, n = w.shape
    nk = pl.cdiv(k_valid, tk)
    assert m % tm == 0 and n % tn == 0 and a_col0 % tk == 0
    assert a.shape[1] >= a_col0 + k_valid
    if epilogue is None:
        epilogue = lambda acc: acc
    ka0 = a_col0 // tk
    in_specs = [pl.BlockSpec((tm, tk), lambda i, j, k: (i, ka0 + k)),
                pl.BlockSpec((tk, tn), lambda i, j, k: (k, j))]
    args = [a, w]
    for arr, col0, *rest in extras:
        row0 = rest[0] if rest else 0
        assert col0 % tn == 0 and row0 % tm == 0
        in_specs.append(pl.BlockSpec((tm, tn), lambda i, j, k, c=col0 // tn, r=row0 // tm: (r + i, c + j)))
        args.append(arr)
    body = functools.partial(_mm_body, nk=nk, tk=tk, k_valid=k_valid, n_extra=len(extras),
                             epilogue=epilogue)
    return pl.pallas_call(
        body,
        grid=(m // tm, n // tn, nk),
        in_specs=in_specs,
        out_specs=pl.BlockSpec((tm, tn), lambda i, j, k: (i, j)),
        out_shape=jax.ShapeDtypeStruct((m, n), out_dtype),
        scratch_shapes=[pltpu.VMEM((tm, tn), F32)],
        compiler_params=_cparams(("parallel", "parallel", "arbitrary")),
        name=name,
    )(*args)


def _rope_tile(x, cs, sn):
    lane = lax.broadcasted_iota(I32, x.shape, x.ndim - 1) & (V7X_LANES - 1)
    half = MLA_ROPE // 2
    rot = jnp.where(lane < half, pltpu.roll(x, V7X_LANES - half, x.ndim - 1),
                    pltpu.roll(x, half, x.ndim - 1))
    return x * cs + rot * sn


def _mla_proj_body(qa_ref, kpe_ref, ckv_ref, cs_ref, sn_ref, qn_ref, wuq_ref, kvn_ref, wk_ref,
                   qlat_ref, qpe_ref, lat_ref):
    qa = qa_ref[...]
    ms = jnp.mean(qa * qa, axis=-1, keepdims=True)
    qan = (qa * lax.rsqrt(ms + RMS_EPS) * qn_ref[...]).astype(BF16)
    q = _dot(qan, wuq_ref[...])
    nope_w = MLA_HEADS * MLA_NOPE
    for h in range(MLA_HEADS):
        qn_h = q[:, h * MLA_NOPE:(h + 1) * MLA_NOPE].astype(BF16)
        qlat_ref[:, h * MLA_KV_LORA:(h + 1) * MLA_KV_LORA] = _dot(qn_h, wk_ref[h])
    cs = cs_ref[...]
    sn = sn_ref[...]
    for h in range(MLA_HEADS):
        sl = slice(h * V7X_LANES, (h + 1) * V7X_LANES)
        qpe_ref[:, sl] = _rope_tile(q[:, nope_w + h * V7X_LANES:nope_w + (h + 1) * V7X_LANES], cs, sn)
    ckv = ckv_ref[...]
    ms2 = jnp.mean(ckv * ckv, axis=-1, keepdims=True)
    lat_ref[:, 0:MLA_KV_LORA] = ckv * lax.rsqrt(ms2 + RMS_EPS) * kvn_ref[...]
    kpe = _rope_tile(kpe_ref[...], cs, sn)
    lat_ref[:, MLA_KV_LORA:MLA_KV_LORA + MLA_ROPE] = kpe[:, 0:MLA_ROPE]


def _mla_project(z, cs, sn, q_norm, w_uq_p, kv_norm, wk, *, tm=256):
    m = z.shape[0]
    lat_w = MLA_KV_LORA + MLA_ROPE
    return pl.pallas_call(
        _mla_proj_body,
        grid=(m // tm,),
        in_specs=[
            pl.BlockSpec((tm, MLA_Q_LORA), lambda i: (i, Z_QA // MLA_Q_LORA)),
            pl.BlockSpec((tm, V7X_LANES), lambda i: (i, Z_KPE // V7X_LANES)),
            pl.BlockSpec((tm, MLA_KV_LORA), lambda i: (i, Z_CKV // MLA_KV_LORA)),
            pl.BlockSpec((tm, V7X_LANES), lambda i: (i, 0)),
            pl.BlockSpec((tm, V7X_LANES), lambda i: (i, 0)),
            pl.BlockSpec((1, MLA_Q_LORA), lambda i: (0, 0)),
            pl.BlockSpec(w_uq_p.shape, lambda i: (0, 0)),
            pl.BlockSpec((1, MLA_KV_LORA), lambda i: (0, 0)),
            pl.BlockSpec(wk.shape, lambda i: (0, 0, 0)),
        ],
        out_specs=[
            pl.BlockSpec((tm, MLA_HEADS * MLA_KV_LORA), lambda i: (i, 0)),
            pl.BlockSpec((tm, MLA_HEADS * V7X_LANES), lambda i: (i, 0)),
            pl.BlockSpec((tm, lat_w), lambda i: (i, 0)),
        ],
        out_shape=[
            jax.ShapeDtypeStruct((m, MLA_HEADS * MLA_KV_LORA), F32),
            jax.ShapeDtypeStruct((m, MLA_HEADS * V7X_LANES), F32),
            jax.ShapeDtypeStruct((m, lat_w), F32),
        ],
        compiler_params=_cparams(("parallel",)),
        name="mla_project",
    )(z, z, z, cs, sn, q_norm.reshape(1, -1), w_uq_p, kv_norm.reshape(1, -1), wk)


def _stack_heads(ref, width, valid=None):
    valid = width if valid is None else valid
    return jnp.concatenate(
        [ref[:, h * width:h * width + valid].astype(BF16) for h in range(MLA_HEADS)], axis=0)


def _mla_prompt_body(qlat_ref, qpe_ref, lat_ref, o_ref, qlat_s, qpe_s, m_s, l_s, acc_s, *, tk):
    qb = pl.program_id(1)
    nq = qlat_ref.shape[0]
    rows = MLA_HEADS * nq
    qlat_s[...] = _stack_heads(qlat_ref, MLA_KV_LORA)
    qpe_s[...] = _stack_heads(qpe_ref, V7X_LANES, MLA_ROPE)
    m_s[...] = jnp.full(m_s.shape, NEG_INF, F32)
    l_s[...] = jnp.zeros(l_s.shape, F32)
    acc_s[...] = jnp.zeros(acc_s.shape, F32)
    qpos = qb * nq + (lax.broadcasted_iota(I32, (rows, tk), 0) & (nq - 1))
    lane = lax.broadcasted_iota(I32, (rows, tk), 1)

    def step(kt, masked):
        k0 = pl.multiple_of(kt * tk, tk)
        kv = lat_ref[pl.ds(k0, tk), :]
        ckv = kv[:, 0:MLA_KV_LORA].astype(BF16)
        kpe = kv[:, MLA_KV_LORA:MLA_KV_LORA + MLA_ROPE].astype(BF16)
        s = (_dot_nt(qlat_s[...], ckv) + _dot_nt(qpe_s[...], kpe)) * MLA_SCALE
        if masked:
            valid = (k0 + lane) <= qpos
            s = jnp.where(valid, s, NEG_INF)
        m_prev = m_s[...]
        m_new = jnp.maximum(m_prev, jnp.max(s, axis=1, keepdims=True))
        alpha = jnp.exp(m_prev - m_new)
        p = jnp.exp(s - m_new)
        if masked:
            p = jnp.where(valid, p, 0.0)
        l_s[...] = alpha * l_s[...] + jnp.sum(p, axis=1, keepdims=True)
        acc_s[...] = alpha * acc_s[...] + _dot(p.astype(BF16), ckv)
        m_s[...] = m_new

    def full_step(kt, carry):
        step(kt, False)
        return carry

    def diag_step(kt, carry):
        step(kt, True)
        return carry

    n_full = (qb * nq) // tk
    n_tiles = ((qb + 1) * nq + tk - 1) // tk
    lax.fori_loop(0, n_full, full_step, 0)
    lax.fori_loop(n_full, n_tiles, diag_step, 0)
    o = acc_s[...] / l_s[...]
    for h in range(MLA_HEADS):
        o_ref[:, h * MLA_KV_LORA:(h + 1) * MLA_KV_LORA] = o[h * nq:(h + 1) * nq, :]


def _mla_prompt(q_lat, q_pe, lat, *, batch, seq, tk=512):
    nqb = seq // Q_BLOCK
    rows = MLA_HEADS * Q_BLOCK
    lat_w = MLA_KV_LORA + MLA_ROPE
    return pl.pallas_call(
        functools.partial(_mla_prompt_body, tk=tk),
        grid=(batch, nqb),
        in_specs=[
            pl.BlockSpec((Q_BLOCK, MLA_HEADS * MLA_KV_LORA), lambda b, q: (b * nqb + q, 0)),
            pl.BlockSpec((Q_BLOCK, MLA_HEADS * V7X_LANES), lambda b, q: (b * nqb + q, 0)),
            pl.BlockSpec((seq, lat_w), lambda b, q: (b, 0)),
        ],
        out_specs=pl.BlockSpec((Q_BLOCK, MLA_HEADS * MLA_KV_LORA), lambda b, q: (b * nqb + q, 0)),
        out_shape=jax.ShapeDtypeStruct((batch * seq, MLA_HEADS * MLA_KV_LORA), F32),
        scratch_shapes=[
            pltpu.VMEM((rows, MLA_KV_LORA), BF16),
            pltpu.VMEM((rows, MLA_ROPE), BF16),
            pltpu.VMEM((rows, 1), F32),
            pltpu.VMEM((rows, 1), F32),
            pltpu.VMEM((rows, MLA_KV_LORA), F32),
        ],
        compiler_params=_cparams(("parallel", "arbitrary")),
        name="mla_prompt",
    )(q_lat, q_pe, lat)


def _page_gather(pt_ref, cache_ref, dst_of, sem, seq, page0, n_pages, start, unroll=1):
    def body(g, carry):
        for u in range(unroll):
            src = cache_ref.at[pt_ref[seq, page0 + g * unroll + u]] if start else cache_ref.at[0]
            cp = pltpu.make_async_copy(src, dst_of(g, u), sem)
            if start:
                cp.start()
            else:
                cp.wait()
        return carry
    lax.fori_loop(0, n_pages // unroll, body, 0)


def _row_pages(buf, slot, rows_per_page):
    return lambda g, u: buf.at[slot, pl.ds((g + u) * rows_per_page, rows_per_page)]


def _mla_sample_body(pt_ref, qlat_ref, qpe_ref, new_ref, cache_ref, o_ref,
                     buf, sem, qlat_s, qpe_s, new_s, m_s, l_s, acc_s,
                     *, n_seq, n_chunks, ppc, page, sub):
    s_id = pl.program_id(0)
    c_id = pl.program_id(1)
    g = s_id * n_chunks + c_id
    slot = g % 2
    ds = qlat_ref.shape[0]
    rows = MLA_HEADS * ds
    pps = sub // page

    def dst_of(slot_):
        return lambda gi, u: buf.at[slot_, gi, :, u * page:(u + 1) * page]

    @pl.when(g == 0)
    def _first():
        _page_gather(pt_ref, cache_ref, dst_of(0), sem.at[0], 0, 0, ppc, True, unroll=pps)

    @pl.when(g + 1 < n_seq * n_chunks)
    def _prefetch():
        g1 = g + 1
        _page_gather(pt_ref, cache_ref, dst_of(g1 % 2), sem.at[g1 % 2], g1 // n_chunks,
                     (g1 % n_chunks) * ppc, ppc, True, unroll=pps)

    @pl.when(c_id == 0)
    def _init():
        qlat_s[...] = _stack_heads(qlat_ref, MLA_KV_LORA)
        qpe_s[...] = _stack_heads(qpe_ref, V7X_LANES, MLA_ROPE)
        m_s[...] = jnp.full(m_s.shape, NEG_INF, F32)
        l_s[...] = jnp.zeros(l_s.shape, F32)
        acc_s[...] = jnp.zeros(acc_s.shape, F32)

    _page_gather(pt_ref, cache_ref, dst_of(slot), sem.at[slot], s_id, 0, ppc, False, unroll=pps)

    def flash(s, valid, pv_of):
        s = s * MLA_SCALE
        if valid is not None:
            s = jnp.where(valid, s, NEG_INF)
        m_prev = m_s[...]
        m_new = jnp.maximum(m_prev, jnp.max(s, axis=1, keepdims=True))
        alpha = jnp.exp(m_prev - m_new)
        p = jnp.exp(s - m_new)
        if valid is not None:
            p = jnp.where(valid, p, 0.0)
        l_s[...] = alpha * l_s[...] + jnp.sum(p, axis=1, keepdims=True)
        acc_s[...] = alpha * acc_s[...] + pv_of(p.astype(BF16))
        m_s[...] = m_new

    def step(j, carry):
        kt = buf[slot, j]
        ckv_t = kt[0:MLA_KV_LORA, :].astype(BF16)
        kpe_t = kt[MLA_KV_LORA:MLA_KV_LORA + MLA_ROPE, :].astype(BF16)
        s = _dot(qlat_s[...], ckv_t) + _dot(qpe_s[...], kpe_t)
        flash(s, None, lambda p: _dot_nt(p, ckv_t))
        return carry

    lax.fori_loop(0, (ppc * page) // sub, step, 0, unroll=True)

    @pl.when(c_id == n_chunks - 1)
    def _finish():
        new_s[...] = jnp.zeros(new_s.shape, F32)
        new_s[0:ds, :] = new_ref[...]
        nk = new_s.shape[0]
        t_q = lax.broadcasted_iota(I32, (rows, nk), 0) & (ds - 1)
        j_k = lax.broadcasted_iota(I32, (rows, nk), 1)
        kv = new_s[...]
        ckv = kv[:, 0:MLA_KV_LORA].astype(BF16)
        kpe = kv[:, MLA_KV_LORA:MLA_KV_LORA + MLA_ROPE].astype(BF16)
        s = _dot_nt(qlat_s[...], ckv) + _dot_nt(qpe_s[...], kpe)
        flash(s, j_k <= t_q, lambda p: _dot(p, ckv))
        o = acc_s[...] / l_s[...]
        for h in range(MLA_HEADS):
            o_ref[:, h * MLA_KV_LORA:(h + 1) * MLA_KV_LORA] = o[h * ds:(h + 1) * ds, :]


def _mla_sample(page_table, q_lat, q_pe, lat, cache_t, *, row0, ds, ppc=32, sub=2048):
    n_seq, n_pages = page_table.shape
    lat_w, page = cache_t.shape[1], cache_t.shape[2]
    assert n_pages % ppc == 0 and (ppc * page) % sub == 0 and sub % page == 0 and row0 % ds == 0
    n_chunks = n_pages // ppc
    rows = MLA_HEADS * ds
    r0 = row0 // ds
    body = functools.partial(_mla_sample_body, n_seq=n_seq, n_chunks=n_chunks, ppc=ppc, page=page,
                             sub=sub)
    grid_spec = pltpu.PrefetchScalarGridSpec(
        num_scalar_prefetch=1,
        grid=(n_seq, n_chunks),
        in_specs=[
            pl.BlockSpec((ds, MLA_HEADS * MLA_KV_LORA), lambda s, c, pt: (r0 + s, 0)),
            pl.BlockSpec((ds, MLA_HEADS * V7X_LANES), lambda s, c, pt: (r0 + s, 0)),
            pl.BlockSpec((ds, lat_w), lambda s, c, pt: (r0 + s, 0)),
            pl.BlockSpec(memory_space=pl.ANY),
        ],
        out_specs=pl.BlockSpec((ds, MLA_HEADS * MLA_KV_LORA), lambda s, c, pt: (s, 0)),
        scratch_shapes=[
            pltpu.VMEM((2, (ppc * page) // sub, lat_w, sub), F32),
            pltpu.SemaphoreType.DMA((2,)),
            pltpu.VMEM((rows, MLA_KV_LORA), BF16),
            pltpu.VMEM((rows, MLA_ROPE), BF16),
            pltpu.VMEM((V7X_LANES, lat_w), F32),
            pltpu.VMEM((rows, 1), F32),
            pltpu.VMEM((rows, 1), F32),
            pltpu.VMEM((rows, MLA_KV_LORA), F32),
        ],
    )
    return pl.pallas_call(
        body,
        grid_spec=grid_spec,
        out_shape=jax.ShapeDtypeStruct((n_seq * ds, MLA_HEADS * MLA_KV_LORA), F32),
        compiler_params=_cparams(("arbitrary", "arbitrary")),
        name="mla_sample",
    )(page_table, q_lat, q_pe, lat, cache_t)


def _mla_uv_body(o_ref, w_ref, out_ref):
    for h in range(MLA_HEADS):
        o_h = o_ref[:, h * MLA_KV_LORA:(h + 1) * MLA_KV_LORA].astype(BF16)
        out_ref[:, h * MLA_V:(h + 1) * MLA_V] = _dot(o_h, w_ref[h]).astype(out_ref.dtype)


def _mla_uv(o_lat, w_uv, *, tm=256, name="mla_uv"):
    m = o_lat.shape[0]
    assert m % tm == 0
    return pl.pallas_call(
        _mla_uv_body,
        grid=(m // tm,),
        in_specs=[pl.BlockSpec((tm, MLA_HEADS * MLA_KV_LORA), lambda i: (i, 0)),
                  pl.BlockSpec(w_uv.shape, lambda i: (0, 0, 0))],
        out_specs=pl.BlockSpec((tm, MLA_HEADS * MLA_V), lambda i: (i, 0)),
        out_shape=jax.ShapeDtypeStruct((m, MLA_HEADS * MLA_V), BF16),
        compiler_params=_cparams(("parallel",)),
        name=name,
    )(o_lat, w_uv)


def _rel_bucket_np(dist):
    n = np.maximum(dist, 0)
    max_exact = REL_BUCKETS // 2
    nf = np.maximum(n, max_exact).astype(np.float32)
    large = max_exact + (np.log(nf / max_exact) / math.log(REL_MAX_DIST / max_exact)
                         * (REL_BUCKETS - max_exact)).astype(np.int32)
    large = np.minimum(large, REL_BUCKETS - 1)
    return np.where(n < max_exact, n, large).astype(np.int32)


def _bias_bank_body(rb_ref, bucket_ref, o_ref):
    h = pl.program_id(0)
    bucket = bucket_ref[...]
    acc = jnp.zeros(bucket.shape, F32)
    for b in range(REL_BUCKETS):
        acc = jnp.where(bucket == b, rb_ref[b, h], acc)
    o_ref[0] = acc


def _bias_bank(rel_bias, bucket, name):
    r, c = bucket.shape
    return pl.pallas_call(
        _bias_bank_body,
        grid=(NSA_HEADS,),
        in_specs=[pl.BlockSpec(memory_space=pltpu.SMEM),
                  pl.BlockSpec((r, c), lambda h: (0, 0))],
        out_specs=pl.BlockSpec((1, r, c), lambda h: (h, 0, 0)),
        out_shape=jax.ShapeDtypeStruct((NSA_HEADS, r, c), F32),
        compiler_params=_cparams(("parallel",)),
        name=name,
    )(rel_bias, jnp.asarray(bucket))


def _gelu_tanh(x):
    return 0.5 * x * (1.0 + jnp.tanh(math.sqrt(2.0 / math.pi) * (x + 0.044715 * x * x * x)))


def _compress_rows(load_rows, n_blocks, phi_ref, w1_ref, w2_ref, flat_s, hi_s, outs, load_chunks=None):
    hd = NSA_HEAD_DIM
    gc = flat_s.shape[1]
    gb = gc - V7X_SUBLANES
    nh1 = w1_ref.shape[2] // 2
    turn = 0
    consts = []
    for c in range(2):
        phi_c = phi_ref[:, c * hd:(c + 1) * hd]
        phi_flat = jnp.concatenate(
            [jnp.concatenate([phi_c[half * CMP_STRIDE + i:half * CMP_STRIDE + i + 1, :]
                              for i in range(CMP_STRIDE)], axis=1) for half in range(2)]
            + [jnp.zeros((V7X_SUBLANES - 2, CMP_STRIDE * hd), F32)], axis=0)
        pw = _dot(phi_flat.astype(BF16), w1_ref[c])
        consts.append(pw[0:1, 0:nh1] + pw[1:2, nh1:2 * nh1])
    for g0 in range(0, n_blocks, gb):
        t = None
        if load_chunks is not None:
            t = jnp.swapaxes(load_chunks(g0, gc), 0, 1)
        for c in range(2):
            slot, turn = turn % 2, turn + 1
            for i in range(CMP_STRIDE):
                x = t[2 * i + c] if t is not None else load_rows(c, g0 * CMP_STRIDE + i, gc, CMP_STRIDE)
                flat_s[slot, :, i * hd:(i + 1) * hd] = x.astype(BF16)
            ab = _dot(flat_s[slot], w1_ref[c])
            hi_s[slot] = ab[:, nh1:2 * nh1]
            hid = _gelu_tanh(ab[0:gb, 0:nh1] + hi_s[slot, 1:gb + 1, :] + consts[c])
            outs[c][0, g0:g0 + gb, :] = _dot(hid.astype(BF16), w2_ref[c].astype(BF16))


def _cmp_prompt_body(x_ref, phi_ref, w1_ref, w2_ref, ck_ref, cv_ref, xpad_s, flat_s, hi_s, *, n_blocks):
    seq = x_ref.shape[0]
    hd = NSA_HEAD_DIM
    for c in range(2):
        xpad_s[c, 0:seq, :] = x_ref[:, c * hd:(c + 1) * hd]
        xpad_s[c, seq:, :] = jnp.zeros((xpad_s.shape[1] - seq, hd), F32)
    load = lambda c, first, count, stride: xpad_s[c, pl.ds(first, count, stride=stride), :]
    _compress_rows(load, n_blocks, phi_ref, w1_ref, w2_ref, flat_s, hi_s, (ck_ref, cv_ref))


def _cmp_prompt(z, phi, w1, w2, *, batch, seq):
    hd = NSA_HEAD_DIM
    n_blocks = seq // CMP_STRIDE
    n_chunks = n_blocks + V7X_SUBLANES
    pad_rows = n_chunks * CMP_STRIDE
    out = jax.ShapeDtypeStruct((batch, n_blocks, hd), F32)
    return pl.pallas_call(
        functools.partial(_cmp_prompt_body, n_blocks=n_blocks),
        grid=(batch,),
        in_specs=[
            pl.BlockSpec((seq, 2 * hd), lambda b: (b, Z_KV // (2 * hd))),
            pl.BlockSpec(phi.shape, lambda b: (0, 0)),
            pl.BlockSpec(w1.shape, lambda b: (0, 0, 0)),
            pl.BlockSpec(w2.shape, lambda b: (0, 0, 0)),
        ],
        out_specs=[pl.BlockSpec((1, n_blocks, hd), lambda b: (b, 0, 0))] * 2,
        out_shape=[out, out],
        scratch_shapes=[pltpu.VMEM((2, pad_rows, hd), F32),
                        pltpu.VMEM((2, n_chunks, CMP_STRIDE * hd), BF16),
                        pltpu.VMEM((2, n_chunks, w1.shape[2] // 2), F32)],
        compiler_params=_cparams(("parallel",)),
        name="nsa_cmp_prompt",
    )(z, phi, w1, w2)


def _cmp_sample_body(pt_ref, new_ref, cache_ref, phi_ref, w1_ref, w2_ref, ck_ref, cv_ref,
                     buf, sem, flat_s, hi_s, *, n_seq, n_pages, page, n_blocks):
    s_id = pl.program_id(0)
    slot = s_id % 2
    past = n_pages * page
    ds = new_ref.shape[0]

    hd = NSA_HEAD_DIM

    @pl.when(s_id == 0)
    def _first():
        _page_gather(pt_ref, cache_ref, _row_pages(buf, 0, 2 * page), sem.at[0], 0, 0, n_pages, True)

    @pl.when(s_id + 1 < n_seq)
    def _prefetch():
        nxt = (s_id + 1) % 2
        _page_gather(pt_ref, cache_ref, _row_pages(buf, nxt, 2 * page), sem.at[nxt], s_id + 1, 0,
                     n_pages, True)

    _store_kv_rows(buf, slot, past, new_ref[:, 0:2 * hd])
    buf[slot, 2 * (past + ds):, :] = jnp.zeros((buf.shape[1] - 2 * (past + ds), hd), F32)
    _page_gather(pt_ref, cache_ref, _row_pages(buf, slot, 2 * page), sem.at[slot], s_id, 0, n_pages, False)
    load = lambda c, first, count, stride: buf[slot, pl.ds(2 * first + c, count, stride=2 * stride), :]
    chunks = lambda g0, gc: buf[slot, pl.ds(2 * CMP_STRIDE * g0, 2 * CMP_STRIDE * gc), :].reshape(
        gc, 2 * CMP_STRIDE, hd)
    _compress_rows(load, n_blocks, phi_ref, w1_ref, w2_ref, flat_s, hi_s, (ck_ref, cv_ref),
                   load_chunks=chunks)


def _store_kv_rows(buf, slot, first_token, kv):
    hd = NSA_HEAD_DIM
    for c in range(2):
        buf[slot, pl.ds(2 * first_token + c, kv.shape[0], stride=2), :] = kv[:, c * hd:(c + 1) * hd]


def _cmp_sample(page_table, z, cache, phi, w1, w2, *, row0, ds):
    n_seq, n_pages = page_table.shape
    page = cache.shape[1] // 2
    hd = NSA_HEAD_DIM
    past = n_pages * page
    n_chunk = -(-(past + ds) // CMP_STRIDE)
    n_blocks = n_chunk - CMP_BLOCK // CMP_STRIDE + 1
    gb = min(n_blocks, 256)
    assert n_blocks % gb == 0 and gb % V7X_SUBLANES == 0 and row0 % ds == 0
    gc = gb + V7X_SUBLANES
    buf_rows = (n_blocks + V7X_SUBLANES) * CMP_STRIDE
    r0 = row0 // ds
    out = jax.ShapeDtypeStruct((n_seq, n_blocks, hd), F32)
    grid_spec = pltpu.PrefetchScalarGridSpec(
        num_scalar_prefetch=1,
        grid=(n_seq,),
        in_specs=[
            pl.BlockSpec((ds, 2 * hd), lambda s, pt: (r0 + s, Z_KV // (2 * hd))),
            pl.BlockSpec(memory_space=pl.ANY),
            pl.BlockSpec(phi.shape, lambda s, pt: (0, 0)),
            pl.BlockSpec(w1.shape, lambda s, pt: (0, 0, 0)),
            pl.BlockSpec(w2.shape, lambda s, pt: (0, 0, 0)),
        ],
        out_specs=[pl.BlockSpec((1, n_blocks, hd), lambda s, pt: (s, 0, 0))] * 2,
        scratch_shapes=[
            pltpu.VMEM((2, 2 * buf_rows, hd), F32),
            pltpu.SemaphoreType.DMA((2,)),
            pltpu.VMEM((2, gc, CMP_STRIDE * hd), BF16),
            pltpu.VMEM((2, gc, w1.shape[2] // 2), F32),
        ],
    )
    body = functools.partial(_cmp_sample_body, n_seq=n_seq, n_pages=n_pages, page=page,
                             n_blocks=n_blocks)
    return pl.pallas_call(
        body,
        grid_spec=grid_spec,
        out_shape=[out, out],
        compiler_params=_cparams(("arbitrary",)),
        name="nsa_cmp_sample",
    )(page_table, z, cache, phi, w1, w2)


def _softmax_rows(s, valid):
    s = jnp.where(valid, s, NEG_INF)
    m = jnp.max(s, axis=-1, keepdims=True)
    p = jnp.where(valid, jnp.exp(s - m), 0.0)
    l = jnp.sum(p, axis=-1, keepdims=True)
    return p / jnp.where(l > 0.0, l, 1.0)


def _flash_update(s, valid, v, m_s, l_s, acc_s):
    nh, nq, tk = s.shape
    s = jnp.where(valid, s, NEG_INF)
    m_prev = m_s[...]
    m_new = jnp.maximum(m_prev, jnp.max(s, axis=-1, keepdims=True))
    alpha = jnp.exp(m_prev - m_new)
    p = jnp.where(valid, jnp.exp(s - m_new), 0.0)
    l_s[...] = alpha * l_s[...] + jnp.sum(p, axis=-1, keepdims=True)
    pv = _dot(p.astype(BF16).reshape(nh * nq, tk), v)
    acc_s[...] = alpha * acc_s[...] + pv.reshape(nh, nq, v.shape[1])
    m_s[...] = m_new


def _flash_reset(m_s, l_s, acc_s):
    m_s[...] = jnp.full(m_s.shape, NEG_INF, F32)
    l_s[...] = jnp.zeros(l_s.shape, F32)
    acc_s[...] = jnp.zeros(acc_s.shape, F32)


def _nsa_prompt_body(q_ref, g_ref, ck_ref, cv_ref, slc_ref, win_ref, bank_ref, bankc_ref,
                     cover_ref, expand_ref, o_ref,
                     qs_s, sel_s, oc_s, m_s, l_s, acc_s, os_s, *, tk, n_slc_blocks):
    qb = pl.program_id(1)
    nq = q_ref.shape[0]
    hd = NSA_HEAD_DIM
    nh = NSA_HEADS
    rows = nh * nq
    start = qb * nq
    scale = hd ** -0.5
    qs_s[...] = jnp.concatenate(
        [(q_ref[:, h * hd:(h + 1) * hd] * scale).astype(BF16) for h in range(nh)], axis=0)

    nc = ck_ref.shape[1]
    s_c = _dot_nt(qs_s[...], ck_ref[0].astype(BF16)) + bankc_ref[...].reshape(rows, nc)
    i_row = lax.broadcasted_iota(I32, (rows, nc), 0) & (nq - 1)
    n_col = lax.broadcasted_iota(I32, (rows, nc), 1)
    valid_c = (n_col * CMP_STRIDE + (CMP_BLOCK - 1)) <= (start + i_row)
    p_c = _softmax_rows(s_c, valid_c)
    oc_s[...] = _dot(p_c.astype(BF16), cv_ref[0].astype(BF16))

    ph = p_c[0:nq, :]
    for h in range(1, nh):
        ph = ph + p_c[h * nq:(h + 1) * nq, :]
    hi, lo = _split_bf16(ph)
    imp = _dot(hi, cover_ref[...]) + _dot(lo, cover_ref[...])
    ns_pad = imp.shape[1]
    blk = lax.broadcasted_iota(I32, (nq, ns_pad), 1)
    qpos = start + lax.broadcasted_iota(I32, (nq, ns_pad), 0)
    cur = lax.shift_right_logical(qpos, int(math.log2(SLC_BLOCK)))
    forced = (blk == 0) | (blk == cur) | (blk == cur - 1)
    visible = (blk * SLC_BLOCK) <= qpos
    score = jnp.where(forced, FORCED_SCORE, jnp.where(visible, imp, -1.0))
    score = jnp.where(blk < n_slc_blocks, score, PAD_SCORE)
    rank = jnp.zeros((nq, ns_pad), F32)
    for sp in range(n_slc_blocks):
        col = score[:, sp:sp + 1]
        ahead = (col > score) | ((col == score) & (blk > sp))
        rank = rank + jnp.where(ahead, 1.0, 0.0)
    sel = jnp.where((rank < float(min(N_SELECT, n_slc_blocks))) & (blk < n_slc_blocks), 1.0, 0.0)
    selx = _dot(sel.astype(BF16), expand_ref[...])
    n_kt = selx.shape[1] // tk
    for kt in range(n_kt):
        sel_s[kt] = selx[:, kt * tk:(kt + 1) * tk]

    i3 = lax.broadcasted_iota(I32, (nq, tk), 0)
    j3 = lax.broadcasted_iota(I32, (nq, tk), 1)
    n_bank = bank_ref.shape[0]
    half = tk // V7X_LANES

    def bias_tile(r0_tiles):
        parts = [bank_ref[jnp.clip(r0_tiles + u, 0, n_bank - 1)] for u in range(half)]
        return jnp.concatenate(parts, axis=1)

    _flash_reset(m_s, l_s, acc_s)
    assert nq == V7X_LANES
    org_tiles = qb - (WINDOW + Q_BLOCK) // V7X_LANES

    def slc_step(kt, carry):
        k0 = pl.multiple_of(kt * tk, tk)
        kv = slc_ref[pl.ds(k0, tk), :]
        s = _dot_nt(qs_s[...], kv[:, 0:hd].astype(BF16))
        s = s + bias_tile(kt * half - org_tiles)
        valid = (sel_s[kt] > 0.5) & ((k0 + j3) <= (start + i3))
        s3 = s.reshape(nh, nq, tk)
        _flash_update(s3, valid[None], kv[:, hd:2 * hd].astype(BF16), m_s, l_s, acc_s)
        return carry

    lax.fori_loop(0, (start + nq + tk - 1) // tk, slc_step, 0)
    os_s[...] = acc_s[...] / l_s[...]

    _flash_reset(m_s, l_s, acc_s)
    pad = WINDOW + Q_BLOCK
    for d in range(-(-(pad + nq) // tk)):
        r0 = d * tk
        kv = win_ref[0, pl.ds(start + r0, tk), :]
        kpos = start - pad + r0 + j3
        s = _dot_nt(qs_s[...], kv[:, 0:hd].astype(BF16)) + bias_tile(d * half)
        qp = start + i3
        valid = (kpos <= qp) & (kpos > qp - WINDOW) & (kpos >= 0)
        _flash_update(s.reshape(nh, nq, tk), valid[None], kv[:, hd:2 * hd].astype(BF16),
                      m_s, l_s, acc_s)
    o_w = acc_s[...] / l_s[...]

    gate = jax.nn.sigmoid(g_ref[...])
    for h in range(nh):
        o_ref[:, h * hd:(h + 1) * hd] = (
            gate[:, 3 * h:3 * h + 1] * oc_s[h * nq:(h + 1) * nq, :]
            + gate[:, 3 * h + 1:3 * h + 2] * os_s[h]
            + gate[:, 3 * h + 2:3 * h + 3] * o_w[h])


def _nsa_prompt(z, cmp_k, cmp_v, win_pad, bank, bank_c, cover, expand, *, batch, seq, tk=512):
    nqb = seq // Q_BLOCK
    hd = NSA_HEAD_DIM
    nh = NSA_HEADS
    rows = nh * Q_BLOCK
    nc = cmp_k.shape[1]
    n_slc_blocks = -(-seq // SLC_BLOCK)
    win_span = -(-(WINDOW + 2 * Q_BLOCK) // tk) * tk
    assert seq % tk == 0 and win_pad.shape[1] >= seq - Q_BLOCK + win_span
    body = functools.partial(_nsa_prompt_body, tk=tk, n_slc_blocks=n_slc_blocks)
    return pl.pallas_call(
        body,
        grid=(batch, nqb),
        in_specs=[
            pl.BlockSpec((Q_BLOCK, nh * hd), lambda b, q: (b * nqb + q, Z_Q // (nh * hd))),
            pl.BlockSpec((Q_BLOCK, V7X_LANES), lambda b, q: (b * nqb + q, Z_G // V7X_LANES)),
            pl.BlockSpec((1, nc, hd), lambda b, q: (b, 0, 0)),
            pl.BlockSpec((1, nc, hd), lambda b, q: (b, 0, 0)),
            pl.BlockSpec((seq, 2 * hd), lambda b, q: (b, Z_KV // (2 * hd) + 1)),
            pl.BlockSpec((1,) + win_pad.shape[1:], lambda b, q: (b, 0, 0)),
            pl.BlockSpec(bank.shape, lambda b, q: (0, 0, 0)),
            pl.BlockSpec((nh, Q_BLOCK, nc), lambda b, q: (0, q, 0)),
            pl.BlockSpec(cover.shape, lambda b, q: (0, 0)),
            pl.BlockSpec(expand.shape, lambda b, q: (0, 0)),
        ],
        out_specs=pl.BlockSpec((Q_BLOCK, nh * hd), lambda b, q: (b * nqb + q, 0)),
        out_shape=jax.ShapeDtypeStruct((batch * seq, nh * hd), F32),
        scratch_shapes=[
            pltpu.VMEM((rows, hd), BF16),
            pltpu.VMEM((seq // tk, Q_BLOCK, tk), F32),
            pltpu.VMEM((rows, hd), F32),
            pltpu.VMEM((nh, Q_BLOCK, 1), F32),
            pltpu.VMEM((nh, Q_BLOCK, 1), F32),
            pltpu.VMEM((nh, Q_BLOCK, hd), F32),
            pltpu.VMEM((nh, Q_BLOCK, hd), F32),
        ],
        compiler_params=_cparams(("parallel", "arbitrary")),
        name="nsa_prompt",
    )(z, z, cmp_k, cmp_v, z, win_pad, bank, bank_c, cover, expand)


def _nsa_sample_body(pt_ref, q_ref, g_ref, new_ref, ck_ref, cv_ref, win_ref, cache_ref,
                     bankc_ref, bankw_ref, banks_ref, bankt_ref, cover_ref, expand_ref,
                     o_ref,
                     buf, sem, qs_s, wkv_s, selx_s, m_s, l_s, acc_s,
                     *, n_seq, n_pages, page, chunk, n_slc_blocks):
    s_id = pl.program_id(0)
    slot = s_id % 2
    ds = q_ref.shape[0]
    hd = NSA_HEAD_DIM
    nh = NSA_HEADS
    rows = nh * ds
    past = n_pages * page
    n_chunks = past // chunk
    scale = hd ** -0.5

    @pl.when(s_id == 0)
    def _first():
        _page_gather(pt_ref, cache_ref, _row_pages(buf, 0, 2 * page), sem.at[0], 0, 0, n_pages, True)

    @pl.when(s_id + 1 < n_seq)
    def _prefetch():
        nxt = (s_id + 1) % 2
        _page_gather(pt_ref, cache_ref, _row_pages(buf, nxt, 2 * page), sem.at[nxt], s_id + 1, 0,
                     n_pages, True)

    new_kv = new_ref[...]
    tail = buf.shape[1] // 2 - past
    _store_kv_rows(buf, slot, past, new_kv[:, 2 * hd:4 * hd])
    buf[slot, 2 * (past + ds):, :] = jnp.zeros((2 * (tail - ds), hd), F32)

    def load_kv(first, count):
        return (buf[slot, pl.ds(2 * first, count, stride=2), :].astype(BF16),
                buf[slot, pl.ds(2 * first + 1, count, stride=2), :].astype(BF16))

    qs = jnp.concatenate(
        [(q_ref[:, h * hd:(h + 1) * hd] * scale).astype(BF16) for h in range(nh)], axis=0)
    qs_s[...] = qs
    t_col = lax.broadcasted_iota(I32, (rows, 1), 0) & (ds - 1)

    nc = ck_ref.shape[1]
    s_c = _dot_nt(qs, ck_ref[0].astype(BF16)) + bankc_ref[...]
    n_col = lax.broadcasted_iota(I32, (rows, nc), 1)
    valid_c = (n_col * CMP_STRIDE + (CMP_BLOCK - 1)) <= (past + t_col)
    p_c = _softmax_rows(s_c, valid_c)
    o_c = _dot(p_c.astype(BF16), cv_ref[0].astype(BF16))

    ph = p_c[0:ds, :]
    for h in range(1, nh):
        ph = ph + p_c[h * ds:(h + 1) * ds, :]
    hi, lo = _split_bf16(ph)
    imp = _dot(hi, cover_ref[...]) + _dot(lo, cover_ref[...])
    ns_pad = imp.shape[1]
    blk = lax.broadcasted_iota(I32, (ds, ns_pad), 1)
    qpos = past + lax.broadcasted_iota(I32, (ds, ns_pad), 0)
    cur = lax.shift_right_logical(qpos, int(math.log2(SLC_BLOCK)))
    forced = (blk == 0) | (blk == cur) | (blk == cur - 1)
    visible = (blk * SLC_BLOCK) <= qpos
    score = jnp.where(forced, FORCED_SCORE, jnp.where(visible, imp, -1.0))
    score = jnp.where(blk < n_slc_blocks, score, PAD_SCORE)
    score_t = jnp.concatenate([score, jnp.zeros((V7X_LANES - ds, ns_pad), F32)], axis=0).T
    sp_idx = lax.broadcasted_iota(I32, (ns_pad, ns_pad), 0)
    s_idx = lax.broadcasted_iota(I32, (ns_pad, ns_pad), 1)
    ranks = []
    for t in range(ds):
        c = score_t[:, t:t + 1]
        r = score[t:t + 1, :]
        ahead = (c > r) | ((c == r) & (sp_idx < s_idx))
        ranks.append(jnp.sum(jnp.where(ahead, 1.0, 0.0), axis=0, keepdims=True))
    rank = jnp.concatenate(ranks, axis=0)
    sel = jnp.where((rank < float(min(N_SELECT, n_slc_blocks))) & (blk < n_slc_blocks), 1.0, 0.0)

    bpc = chunk // SLC_BLOCK
    n_groups = selx_s.shape[0] // ds
    lane = lax.broadcasted_iota(I32, (ds, V7X_LANES), 1)
    pieces = []
    for gi in range(n_groups):
        a, off = (gi * bpc) // V7X_LANES, (gi * bpc) % V7X_LANES
        tile = sel[:, a * V7X_LANES:(a + 1) * V7X_LANES]
        if off:
            tile = pltpu.roll(tile, V7X_LANES - off, 1)
        pieces.append(jnp.where(lane < bpc, tile, 0.0))
    selx_s[...] = _dot(jnp.concatenate(pieces, axis=0).astype(BF16), expand_ref[...])

    _page_gather(pt_ref, cache_ref, _row_pages(buf, slot, 2 * page), sem.at[slot], s_id, 0, n_pages, False)
    _flash_reset(m_s, l_s, acc_s)
    far_bias = banks_ref[:, 0:1].reshape(nh, ds, 1)

    def slc_chunk(c, bias):
        k, v = load_kv(pl.multiple_of(c * chunk, chunk), chunk)
        s = _dot_nt(qs_s[...], k).reshape(nh, ds, chunk) + bias
        valid = selx_s[pl.ds(pl.multiple_of(c * ds, ds), ds), :] > 0.5
        _flash_update(s, valid[None], v, m_s, l_s, acc_s)

    def slc_step(c, carry):
        slc_chunk(c, far_bias)
        return carry

    lax.fori_loop(0, n_chunks - 1, slc_step, 0, unroll=True)
    slc_chunk(n_chunks - 1, banks_ref[...].reshape(nh, ds, chunk))

    k_t, v_t = load_kv(past, tail)
    s_t = _dot_nt(qs, k_t) + bankt_ref[...]
    j_t = lax.broadcasted_iota(I32, (ds, tail), 1)
    t_t = lax.broadcasted_iota(I32, (ds, tail), 0)
    valid_t = (selx_s[n_chunks * ds:(n_chunks + 1) * ds, 0:tail] > 0.5) & (j_t <= t_t)
    _flash_update(s_t.reshape(nh, ds, tail), valid_t[None], v_t, m_s, l_s, acc_s)
    o_s = acc_s[...] / l_s[...]

    wb = win_ref.shape[1] // 2
    nw = wkv_s.shape[1]
    for c in range(2):
        wkv_s[c, 0:wb, :] = win_ref[0, pl.ds(c, wb, stride=2), :]
        wkv_s[c, wb:wb + ds, :] = new_kv[:, (4 + c) * hd:(5 + c) * hd]
        wkv_s[c, wb + ds:, :] = jnp.zeros((nw - wb - ds, hd), F32)
    s_w = _dot_nt(qs, wkv_s[0].astype(BF16)) + bankw_ref[...]
    w_col = lax.broadcasted_iota(I32, (rows, nw), 1)
    valid_w = ((w_col - wb) <= t_col) & ((w_col - wb) > (t_col - WINDOW)) & (w_col < wb + ds)
    p_w = _softmax_rows(s_w, valid_w)
    o_w = _dot(p_w.astype(BF16), wkv_s[1].astype(BF16))

    gate = jax.nn.sigmoid(g_ref[...])
    for h in range(nh):
        o_ref[:, h * hd:(h + 1) * hd] = (
            gate[:, 3 * h:3 * h + 1] * o_c[h * ds:(h + 1) * ds, :]
            + gate[:, 3 * h + 1:3 * h + 2] * o_s[h]
            + gate[:, 3 * h + 2:3 * h + 3] * o_w[h * ds:(h + 1) * ds, :])


def _nsa_sample(page_table, z, cmp_k, cmp_v, win_buf, cache, bank_c, bank_w, bank_s, bank_t,
                cover, expand, *, row0, ds, chunk=2048):
    n_seq, n_pages = page_table.shape
    page = cache.shape[1] // 2
    hd = NSA_HEAD_DIM
    nh = NSA_HEADS
    rows = nh * ds
    past = n_pages * page
    assert past % chunk == 0 and row0 % ds == 0 and chunk // SLC_BLOCK <= V7X_LANES
    n_slc_blocks = -(-(past + ds) // SLC_BLOCK)
    tail = V7X_LANES
    n_groups = past // chunk + 1
    nc = cmp_k.shape[1]
    wb2 = win_buf.shape[1]
    nw = bank_w.shape[1]
    r0 = row0 // ds
    body = functools.partial(_nsa_sample_body, n_seq=n_seq, n_pages=n_pages, page=page,
                             chunk=chunk, n_slc_blocks=n_slc_blocks)
    const2 = lambda s, pt: (0, 0)
    grid_spec = pltpu.PrefetchScalarGridSpec(
        num_scalar_prefetch=1,
        grid=(n_seq,),
        in_specs=[
            pl.BlockSpec((ds, nh * hd), lambda s, pt: (r0 + s, Z_Q // (nh * hd))),
            pl.BlockSpec((ds, V7X_LANES), lambda s, pt: (r0 + s, Z_G // V7X_LANES)),
            pl.BlockSpec((ds, 6 * hd), lambda s, pt: (r0 + s, Z_KV // (6 * hd))),
            pl.BlockSpec((1, nc, hd), lambda s, pt: (s, 0, 0)),
            pl.BlockSpec((1, nc, hd), lambda s, pt: (s, 0, 0)),
            pl.BlockSpec((1, wb2, hd), lambda s, pt: (s, 0, 0)),
            pl.BlockSpec(memory_space=pl.ANY),
            pl.BlockSpec(bank_c.shape, const2),
            pl.BlockSpec(bank_w.shape, const2),
            pl.BlockSpec(bank_s.shape, const2),
            pl.BlockSpec(bank_t.shape, const2),
            pl.BlockSpec(cover.shape, const2),
            pl.BlockSpec(expand.shape, const2),
        ],
        out_specs=pl.BlockSpec((ds, nh * hd), lambda s, pt: (s, 0)),
        scratch_shapes=[
            pltpu.VMEM((2, 2 * (past + tail), hd), F32),
            pltpu.SemaphoreType.DMA((2,)),
            pltpu.VMEM((rows, hd), BF16),
            pltpu.VMEM((2, nw, hd), F32),
            pltpu.VMEM((n_groups * ds, chunk), F32),
            pltpu.VMEM((nh, ds, 1), F32),
            pltpu.VMEM((nh, ds, 1), F32),
            pltpu.VMEM((nh, ds, hd), F32),
        ],
    )
    return pl.pallas_call(
        body,
        grid_spec=grid_spec,
        out_shape=jax.ShapeDtypeStruct((n_seq * ds, nh * hd), F32),
        compiler_params=_cparams(("arbitrary",)),
        name="nsa_sample",
    )(page_table, z, z, z, cmp_k, cmp_v, win_buf, cache, bank_c, bank_w, bank_s, bank_t,
      cover, expand)


def _conv_silu(g, g1, g2, u, cw_ref, cb_ref, col0, n_valid):
    cw = cw_ref[...]
    conv = cb_ref[...] + cw[0:1] * g2 + cw[1:2] * g1 + cw[2:3] * g
    act = jax.nn.silu(conv) * u
    col = col0 + lax.broadcasted_iota(I32, act.shape, act.ndim - 1)
    return jnp.where(col < n_valid, act, 0.0)


def _ffn_up_prompt_body(a_ref, wg_ref, wu_ref, cw_ref, cb_ref, act_ref, tail_ref,
                        wg_s, wu_s, carry_s, *, tiles_per_seq, n_valid):
    n = pl.program_id(0)
    m = pl.program_id(1)
    tm, tn = act_ref.shape

    @pl.when(m == 0)
    def _cast():
        wg_s[...] = wg_ref[...].astype(BF16)
        wu_s[...] = wu_ref[...].astype(BF16)

    @pl.when(m % tiles_per_seq == 0)
    def _seq_start():
        carry_s[...] = jnp.zeros_like(carry_s)

    a = a_ref[...]
    g = _dot(a, wg_s[...])
    u = _dot(a, wu_s[...])
    prev = carry_s[...]
    row = lax.broadcasted_iota(I32, (tm, tn), 0)
    last = V7X_SUBLANES - 1
    g1 = jnp.where(row == 0, prev[last:last + 1], pltpu.roll(g, 1, 0))
    g2 = jnp.where(row == 0, prev[last - 1:last],
                   jnp.where(row == 1, prev[last:last + 1], pltpu.roll(g, 2, 0)))
    act_ref[...] = _conv_silu(g, g1, g2, u, cw_ref, cb_ref, n * tn, n_valid).astype(act_ref.dtype)
    carry_s[...] = g[tm - V7X_SUBLANES:tm, :]
    tail_ref[0] = g[tm - V7X_SUBLANES:tm, :]


def _ffn_up_prompt(hn, w_gate, w_up, conv_w, conv_b, *, rows, seq, tm=1024, tn=512):
    d, dff = w_gate.shape
    n_blocks = pl.cdiv(dff, tn)
    assert rows % tm == 0 and seq % tm == 0
    body = functools.partial(_ffn_up_prompt_body, tiles_per_seq=seq // tm, n_valid=dff)
    once = pl.Buffered(1)
    return pl.pallas_call(
        body,
        grid=(n_blocks, rows // tm),
        in_specs=[
            pl.BlockSpec((tm, d), lambda n, m: (m, 0)),
            pl.BlockSpec((d, tn), lambda n, m: (0, n), pipeline_mode=once),
            pl.BlockSpec((d, tn), lambda n, m: (0, n), pipeline_mode=once),
            pl.BlockSpec((conv_w.shape[0], tn), lambda n, m: (0, n)),
            pl.BlockSpec((1, tn), lambda n, m: (0, n)),
        ],
        out_specs=[
            pl.BlockSpec((tm, tn), lambda n, m: (m, n)),
            pl.BlockSpec((1, V7X_SUBLANES, tn), lambda n, m: (m, 0, n)),
        ],
        out_shape=[
            jax.ShapeDtypeStruct((rows, dff), BF16),
            jax.ShapeDtypeStruct((rows // tm, V7X_SUBLANES, dff), F32),
        ],
        scratch_shapes=[pltpu.VMEM((d, tn), BF16), pltpu.VMEM((d, tn), BF16),
                        pltpu.VMEM((V7X_SUBLANES, tn), F32)],
        compiler_params=_cparams(("arbitrary", "arbitrary")),
        name="ffn_up_prompt",
    )(hn, w_gate, w_up, conv_w, conv_b.reshape(1, dff))


def _ffn_up_sample_body(a_ref, wg_ref, wu_ref, cw_ref, cb_ref, st_ref, act_ref, st_out_ref,
                        *, ds, n_valid):
    n = pl.program_id(0)
    tm, tn = act_ref.shape
    nseq = tm // ds
    a = a_ref[...]
    g = _dot(a, wg_ref[...].astype(BF16)).reshape(nseq, ds, tn)
    u = _dot(a, wu_ref[...].astype(BF16)).reshape(nseq, ds, tn)
    st = st_ref[...]
    s0, s1 = st[:, 0:1, :], st[:, 1:2, :]
    t = lax.broadcasted_iota(I32, (nseq, ds, tn), 1)
    g1 = jnp.where(t == 0, s1, pltpu.roll(g, 1, 1))
    g2 = jnp.where(t == 0, s0, jnp.where(t == 1, s1, pltpu.roll(g, 2, 1)))
    cw = cw_ref[...]
    conv = cb_ref[...] + cw[0:1] * g2 + cw[1:2] * g1 + cw[2:3] * g
    act = jax.nn.silu(conv) * u
    col = n * tn + lax.broadcasted_iota(I32, act.shape, 2)
    act = jnp.where(col < n_valid, act, 0.0)
    act_ref[...] = act.reshape(tm, tn).astype(act_ref.dtype)
    st_out_ref[...] = g[:, ds - 2:ds, :]


def _ffn_up_sample(hn, w_gate, w_up, conv_w, conv_b, state, *, row0, rows, ds, tn=256):
    d, dff = w_gate.shape
    n_blocks = pl.cdiv(dff, tn)
    assert row0 % rows == 0 and rows % ds == 0 and ds == V7X_SUBLANES
    nseq = rows // ds
    r0 = row0 // rows
    body = functools.partial(_ffn_up_sample_body, ds=ds, n_valid=dff)
    return pl.pallas_call(
        body,
        grid=(n_blocks,),
        in_specs=[
            pl.BlockSpec((rows, d), lambda n: (r0, 0)),
            pl.BlockSpec((d, tn), lambda n: (0, n)),
            pl.BlockSpec((d, tn), lambda n: (0, n)),
            pl.BlockSpec((conv_w.shape[0], tn), lambda n: (0, n)),
            pl.BlockSpec((1, tn), lambda n: (0, n)),
            pl.BlockSpec((nseq, state.shape[1], tn), lambda n: (0, 0, n)),
        ],
        out_specs=[
            pl.BlockSpec((rows, tn), lambda n: (0, n)),
            pl.BlockSpec((nseq, state.shape[1], tn), lambda n: (0, 0, n)),
        ],
        out_shape=[
            jax.ShapeDtypeStruct((rows, dff), BF16),
            jax.ShapeDtypeStruct(state.shape, F32),
        ],
        compiler_params=_cparams(("parallel",)),
        name="ffn_up_sample",
    )(hn, w_gate, w_up, conv_w, conv_b.reshape(1, dff), state)


def _layout_w_in(w_in):
    d = w_in.shape[0]
    nq = NSA_HEADS * NSA_HEAD_DIM
    nkv = 6 * NSA_HEAD_DIM
    ng = 3 * NSA_HEADS
    o_kv, o_g = nq, nq + nkv
    o_qa = o_g + ng
    o_kva = o_qa + MLA_Q_LORA
    o_m = o_kva + MLA_KV_LORA + MLA_ROPE
    d_model = (w_in.shape[1] - o_m) // 2

    def zeros(n):
        return jnp.zeros((d, n), w_in.dtype)

    parts = [
        w_in[:, 0:nq],
        w_in[:, o_g:o_g + ng], zeros(Z_QA - Z_G - ng),
        w_in[:, o_qa:o_qa + MLA_Q_LORA],
        w_in[:, o_kv:o_kv + nkv],
        w_in[:, o_kva + MLA_KV_LORA:o_kva + MLA_KV_LORA + MLA_ROPE],
        zeros(Z_CKV - Z_KPE - MLA_ROPE),
        w_in[:, o_kva:o_kva + MLA_KV_LORA],
        w_in[:, o_m:o_m + 2 * d_model],
    ]
    out = jnp.concatenate(parts, axis=1).astype(BF16)
    assert out.shape[1] == Z_WIDTH and Z_MB - Z_MA == d_model
    return out


def _rope_tables(pos):
    half = MLA_ROPE // 2
    inv = ROPE_THETA ** (-jnp.arange(half, dtype=F32) / half)
    ang = pos.astype(F32)[:, None] * inv
    cos, sin = jnp.cos(ang), jnp.sin(ang)
    pad = jnp.zeros((pos.shape[0], V7X_LANES - MLA_ROPE), F32)
    return (jnp.concatenate([cos, cos, pad], axis=1), jnp.concatenate([-sin, sin, pad], axis=1))


def kernel(x_prompt, x_sample, cache_mla, cache_nsa_cmp_kv, cache_nsa_slc_kv, cache_nsa_win_kv,
           state_ffn_conv, page_table, rel_bias, norm_attn, w_in, nsa_phi_pos, nsa_phi_w1, nsa_phi_w2,
           mla_q_norm, mla_w_uq, mla_kv_norm, mla_w_ukv, w_br_nsa, w_br_mla, w_out, norm_ffn,
           w_ffn_gate, w_ffn_up, ffn_conv_w, ffn_conv_b, w_ffn_down, norm_final):
    depth = norm_attn.shape[0]
    assert depth == 1, "the layer loop is written for the single-layer configuration"
    lyr = 0
    b, seq, d = x_prompt.shape
    db, ds, _ = x_sample.shape
    mp, msm = b * seq, db * ds
    m = mp + msm
    n_pool, page = cache_mla.shape[1], cache_mla.shape[2]
    n_pages = page_table.shape[1]
    past = n_pages * page
    hd = NSA_HEAD_DIM
    nh = NSA_HEADS
    assert seq % Q_BLOCK == 0 and ds == V7X_SUBLANES and past >= WINDOW
    assert cache_nsa_win_kv.shape[2] == WINDOW and seq >= WINDOW

    x_all = jnp.concatenate([x_prompt.reshape(mp, d), x_sample.reshape(msm, d)], axis=0)
    xn = _rmsnorm(x_all, norm_attn[lyr], BF16, name="norm_attn")
    z = _matmul(xn, _layout_w_in(w_in[lyr]), tm=1024, tn=1280, tk=2048, out_dtype=F32, name="in_proj")

    pos_all = jnp.concatenate([jnp.tile(jnp.arange(seq, dtype=I32), b),
                               jnp.tile(past + jnp.arange(ds, dtype=I32), db)])
    cs, sn = _rope_tables(pos_all)
    w_uq = mla_w_uq[lyr].reshape(MLA_Q_LORA, MLA_HEADS, MLA_NOPE + MLA_ROPE)
    w_uq_p = jnp.concatenate([
        w_uq[:, :, :MLA_NOPE].reshape(MLA_Q_LORA, MLA_HEADS * MLA_NOPE),
        jnp.pad(w_uq[:, :, MLA_NOPE:], ((0, 0), (0, 0), (0, V7X_LANES - MLA_ROPE))).reshape(
            MLA_Q_LORA, MLA_HEADS * V7X_LANES)], axis=1).astype(BF16)
    w_ukv = mla_w_ukv[lyr]
    wk = jnp.transpose(w_ukv[:, :, :MLA_NOPE], (1, 2, 0)).astype(BF16)
    w_uv = jnp.transpose(w_ukv[:, :, MLA_NOPE:], (1, 0, 2)).astype(BF16)
    q_lat, q_pe, lat = _mla_project(z, cs, sn, mla_q_norm[lyr], w_uq_p, mla_kv_norm[lyr], wk)

    pt = page_table + lyr * n_pool
    lat_w = MLA_KV_LORA + MLA_ROPE
    o_lat_p = _mla_prompt(q_lat, q_pe, lat, batch=b, seq=seq)
    cache_mla_t = jnp.swapaxes(cache_mla, 2, 3).reshape(depth * n_pool, lat_w, page)
    o_lat_s = _mla_sample(pt, q_lat, q_pe, lat, cache_mla_t, row0=mp, ds=ds)
    o_mla = jnp.concatenate([_mla_uv(o_lat_p, w_uv, name="mla_uv_prompt"),
                             _mla_uv(o_lat_s, w_uv, name="mla_uv_sample")], axis=0)

    pad = WINDOW + Q_BLOCK
    i_q = np.arange(Q_BLOCK)[:, None]
    bank = _bias_bank(rel_bias, _rel_bucket_np(i_q + pad - np.arange(8 * V7X_LANES)[None, :]), "bias_bank_prompt")
    bank = jnp.transpose(bank.reshape(nh, Q_BLOCK, 8, V7X_LANES), (2, 0, 1, 3)).reshape(
        8, nh * Q_BLOCK, V7X_LANES)
    nc_p = seq // CMP_STRIDE
    qpos_p = np.arange(seq)[:, None]
    bank_cp = _bias_bank(rel_bias, _rel_bucket_np(
        qpos_p - (np.arange(nc_p)[None, :] * CMP_STRIDE + CMP_BLOCK - 1)), "bias_bank_prompt_cmp")
    ns_p = -(-seq // SLC_BLOCK)

    def cover_matrix(n_cmp, n_cmp_pad, n_slc, n_slc_pad):
        c_start = np.arange(n_cmp_pad)[:, None] * CMP_STRIDE
        s_start = np.arange(n_slc_pad)[None, :] * SLC_BLOCK
        cov = (c_start < s_start + SLC_BLOCK) & (c_start + CMP_BLOCK > s_start)
        cov &= (np.arange(n_cmp_pad)[:, None] < n_cmp) & (np.arange(n_slc_pad)[None, :] < n_slc)
        return jnp.asarray(cov.astype(np.float32), dtype=BF16)

    cover_p = cover_matrix(seq // CMP_STRIDE - 1, nc_p, ns_p, V7X_LANES)
    expand_p = jnp.asarray(
        (np.arange(seq)[None, :] // SLC_BLOCK == np.arange(V7X_LANES)[:, None]).astype(np.float32), dtype=BF16)
    phi = nsa_phi_pos[lyr].reshape(CMP_BLOCK, 2 * hd)
    half = CMP_STRIDE * hd
    w1 = jnp.concatenate([nsa_phi_w1[lyr][:, :half], nsa_phi_w1[lyr][:, half:]], axis=2).astype(BF16)
    w2 = nsa_phi_w2[lyr]
    cmp_k, cmp_v = _cmp_prompt(z, phi, w1, w2, batch=b, seq=seq)
    win_new = z[:, Z_KV + 4 * hd:Z_KV + 6 * hd]
    tk_nsa = 512
    win_pad = jnp.pad(win_new[:mp].reshape(b, seq, 2 * hd),
                      ((0, 0), (pad, -(pad + Q_BLOCK) % tk_nsa), (0, 0)))
    o_nsa_p = _nsa_prompt(z, cmp_k, cmp_v, win_pad, bank, bank_cp, cover_p, expand_p, batch=b, seq=seq,
                          tk=tk_nsa)

    cache_cmp = cache_nsa_cmp_kv.reshape(depth * n_pool, 2 * page, hd)
    cache_slc = cache_nsa_slc_kv.reshape(depth * n_pool, 2 * page, hd)
    cmp_ks, cmp_vs = _cmp_sample(pt, z, cache_cmp, phi, w1, w2, row0=mp, ds=ds)
    nc_s = cmp_ks.shape[1]
    chunk = 2048
    t_q = np.arange(ds)[:, None]
    wb = cache_nsa_win_kv.shape[2]
    nw = -(-(wb + ds) // V7X_LANES) * V7X_LANES

    def sample_bank(dist, name):
        return _bias_bank(rel_bias, _rel_bucket_np(dist), name).reshape(nh * ds, dist.shape[1])

    bank_cs = sample_bank(past + t_q - (np.arange(nc_s)[None, :] * CMP_STRIDE + CMP_BLOCK - 1), "bias_bank_sample_cmp")
    bank_ws = sample_bank(t_q + wb - np.arange(nw)[None, :], "bias_bank_sample_win")
    bank_ss = sample_bank(past + t_q - (past - chunk + np.arange(chunk)[None, :]), "bias_bank_sample_slc")
    bank_ts = sample_bank(t_q - np.arange(V7X_LANES)[None, :], "bias_bank_sample_new")
    ns_s = -(-(past + ds) // SLC_BLOCK)
    ns_s_pad = -(-ns_s // V7X_LANES) * V7X_LANES
    n_cmp_s = -(-(past + ds) // CMP_STRIDE) - CMP_BLOCK // CMP_STRIDE + 1
    cover_s = cover_matrix(n_cmp_s, nc_s, ns_s, ns_s_pad)
    expand_s = jnp.asarray(
        (np.arange(chunk)[None, :] // SLC_BLOCK == np.arange(V7X_LANES)[:, None]).astype(np.float32), dtype=BF16)
    o_nsa_s = _nsa_sample(pt, z, cmp_ks, cmp_vs, cache_nsa_win_kv[lyr].reshape(db, 2 * wb, hd), cache_slc,
                          bank_cs, bank_ws, bank_ss, bank_ts, cover_s, expand_s,
                          row0=mp, ds=ds, chunk=chunk)
    o_nsa = jnp.concatenate([o_nsa_p, o_nsa_s], axis=0)

    t1 = _matmul(o_nsa, w_br_nsa[lyr], tm=1024, tn=512, tk=2048, out_dtype=F32, extras=[(z, Z_MA)],
                 epilogue=lambda acc, zm: jax.nn.sigmoid(zm) * acc, name="merge_nsa")
    y = _matmul(o_mla, w_br_mla[lyr], tm=1024, tn=512, tk=2048, out_dtype=BF16,
                extras=[(z, Z_MB), (t1, 0)],
                epilogue=lambda acc, zm, t: t + jax.nn.sigmoid(zm) * acc, name="merge_mla")
    h1 = _matmul(y, w_out[lyr], tm=1024, tn=1024, tk=2048, out_dtype=F32, extras=[(x_all, 0)],
                 epilogue=lambda acc, x: x + acc, name="out_proj")

    hn = _rmsnorm(h1, norm_ffn[lyr], BF16, name="norm_ffn")
    dff = w_ffn_gate.shape[2]
    assert ffn_conv_w.shape[1] == 3 and state_ffn_conv.shape[2] == 2
    tm_f = 1024
    act_p, tails = _ffn_up_prompt(hn, w_ffn_gate[lyr], w_ffn_up[lyr], ffn_conv_w[lyr], ffn_conv_b[lyr],
                                  rows=mp, seq=seq, tm=tm_f)
    act_s, conv_s = _ffn_up_sample(hn, w_ffn_gate[lyr], w_ffn_up[lyr], ffn_conv_w[lyr], ffn_conv_b[lyr],
                                   state_ffn_conv[lyr], row0=mp, rows=msm, ds=ds)
    add = lambda acc, h: h + acc
    h2_p = _matmul(act_p, w_ffn_down[lyr], tm=1024, tn=1024, tk=1024, out_dtype=F32, extras=[(h1, 0)],
                   epilogue=add, name="ffn_down_prompt")
    h2_s = _matmul(act_s, w_ffn_down[lyr], tm=1024, tn=1024, tk=1024, out_dtype=F32, extras=[(h1, 0, mp)],
                   epilogue=add, name="ffn_down_sample")
    y_p = _rmsnorm(h2_p, norm_final, F32, name="norm_final_prompt").reshape(b, seq, d)
    y_s = _rmsnorm(h2_s, norm_final, F32, name="norm_final_sample").reshape(db, ds, d)

    kv_tail = (2, 1, hd)
    kv_new = z[:, Z_KV:Z_KV + 6 * hd]
    kv_p = kv_new[:mp].reshape(b, seq, 3, *kv_tail)
    kv_s = kv_new[mp:].reshape(db, ds, 3, *kv_tail)
    win_all = jnp.concatenate([cache_nsa_win_kv[lyr], kv_s[:, :, 2]], axis=1)
    tps = seq // tm_f
    conv_p = tails.reshape(b, tps, V7X_SUBLANES, dff)[:, tps - 1, V7X_SUBLANES - 2:, :]
    return (y_p, y_s,
            lat[:mp].reshape(1, b, seq, lat_w), lat[mp:].reshape(1, db, ds, lat_w),
            kv_p[None, :, :, 0], kv_s[None, :, :, 0],
            kv_p[None, :, :, 1], kv_s[None, :, :, 1],
            kv_p[None, :, seq - min(WINDOW, seq):, 2],
            win_all[None, :, wb + ds - min(WINDOW, wb + ds):],
            conv_p[None], conv_s[None])
```
